```python
import jax, jax.numpy as jnp
from jax import lax
import numpy as np

D_MODEL = 2048
BATCH = 8
SEQ = 4096
DEPTH = 1
DEC_BATCH = 8
DEC_SEQ = 32
PAST_LEN = 4096

CHUNK = 64
CONV_DIM = D_MODEL // 2
CONV_WIDTH = 31
GLA_HEADS = 4
GLA_DK = D_MODEL // 16
GLA_DV = D_MODEL // 8
GLA_RANK = 16
GLA_TAU = 16.0
QK_DIM = GLA_HEADS * GLA_DK
V_DIM = GLA_HEADS * GLA_DV
IN_COLS = 2 * CONV_DIM + 2 * QK_DIM + 2 * V_DIM + GLA_RANK
PEER_HEADS = 8
PEER_NKEYS = 128
PEER_TOPK = 16
PEER_DKEY = 256
PEER_BLOCK = 128
N_EXPERTS = PEER_NKEYS * PEER_NKEYS
LN_EPS = 1e-5
ALPHA = (2.0 * DEPTH) ** 0.25
BETA = (8.0 * DEPTH) ** -0.25

kernel_name = 'hymba_conformer_gla_peer_stream'


def layer_norm(x, g, b):
    xf = x.astype(jnp.float32)
    mu = jnp.mean(xf, axis=-1, keepdims=True)
    var = jnp.mean(jnp.square(xf - mu), axis=-1, keepdims=True)
    return ((xf - mu) * lax.rsqrt(var + LN_EPS) * g + b).astype(x.dtype)


def split_in(z):
    sizes = (CONV_DIM, CONV_DIM, QK_DIM, QK_DIM, V_DIM, V_DIM, GLA_RANK)
    offs = [int(o) for o in np.cumsum(sizes)[:-1]]
    return jnp.split(z, offs, axis=-1)


def conv_mixer(a, gate, conv_buf, conv_w, conv_b, ln_g, ln_b):
    u = a * jax.nn.sigmoid(gate)
    ext = jnp.concatenate([conv_buf.astype(u.dtype), u], axis=1)
    y = lax.conv_general_dilated(ext, conv_w[:, None, :].astype(u.dtype), window_strides=(1,),
                                 padding='VALID', dimension_numbers=('NWC', 'WIO', 'NWC'),
                                 feature_group_count=CONV_DIM) + conv_b
    y = jax.nn.silu(layer_norm(y, ln_g, ln_b))
    return y, ext[:, -(CONV_WIDTH - 1):]


def gla_scan(q, k, v, log_a, s0):
    B, H, L, _ = q.shape
    n = -(-L // CHUNK)
    pad = n * CHUNK - L
    padf = lambda t: jnp.pad(t, ((0, 0), (0, 0), (0, pad), (0, 0)))
    to_chunks = lambda t: jnp.moveaxis(padf(t).reshape(B, H, n, CHUNK, t.shape[-1]), 2, 0)
    qc, kc, vc, ac = to_chunks(q), to_chunks(k), to_chunks(v), to_chunks(log_a)
    causal = jnp.tril(jnp.ones((CHUNK, CHUNK), dtype=bool))

    def step(S, inp):
        qb, kb, vb, ab = inp
        bcum = jnp.cumsum(ab, axis=2)
        diff = bcum[:, :, :, None, :] - bcum[:, :, None, :, :]
        decay = jnp.exp(jnp.where(causal[:, :, None], diff, -jnp.inf))
        scores = jnp.einsum('bhtk,bhsk,bhtsk->bhts', qb, kb, decay)
        o = (jnp.einsum('bhts,bhsv->bhtv', scores, vb)
             + jnp.einsum('bhtk,bhkv->bhtv', qb * jnp.exp(bcum), S))
        b_last = bcum[:, :, -1:, :]
        S_new = (jnp.exp(b_last[:, :, 0, :])[..., None] * S
                 + jnp.einsum('bhsk,bhsv->bhkv', kb * jnp.exp(b_last - bcum), vb))
        return S_new, o

    S, o = lax.scan(step, s0, (qc, kc, vc, ac))
    o = jnp.moveaxis(o, 0, 2).reshape(B, H, n * CHUNK, -1)[:, :, :L]
    return o, S


def gla_mixer(q, k, v, g, f_lr, s0, gate_w, gate_b, norm_g):
    B, L, _ = q.shape
    heads = lambda t, d: t.reshape(B, L, GLA_HEADS, d).transpose(0, 2, 1, 3).astype(jnp.float32)
    log_a = jax.nn.log_sigmoid((f_lr @ gate_w + gate_b).astype(jnp.float32)) / GLA_TAU
    o, S = gla_scan(heads(q, GLA_DK) * (GLA_DK ** -0.5), heads(k, GLA_DK), heads(v, GLA_DV),
                    heads(log_a, GLA_DK), s0.astype(jnp.float32))
    o = o * lax.rsqrt(jnp.mean(o * o, axis=-1, keepdims=True) + LN_EPS)
    o = o.transpose(0, 2, 1, 3).reshape(B, L, V_DIM) * norm_g
    return (o * jax.nn.silu(g.astype(jnp.float32))).astype(q.dtype), S.astype(s0.dtype)


def peer(x, w_query, sub_keys, u_table, v_table):
    B, L, D = x.shape
    T = B * L
    n = -(-T // PEER_BLOCK)
    tp = n * PEER_BLOCK
    xt = jnp.pad(x.reshape(T, D), ((0, tp - T), (0, 0)))
    q = (xt @ w_query).reshape(tp, PEER_HEADS, 2, PEER_DKEY // 2)
    s = jnp.einsum('thpc,hpnc->thpn', q, sub_keys).astype(jnp.float32)
    sv, si = lax.top_k(s, PEER_TOPK)
    cand = sv[:, :, 0, :, None] + sv[:, :, 1, None, :]
    cid = si[:, :, 0, :, None] * PEER_NKEYS + si[:, :, 1, None, :]
    best, pos = lax.top_k(cand.reshape(tp, PEER_HEADS, PEER_TOPK * PEER_TOPK), PEER_TOPK)
    ids = jnp.take_along_axis(cid.reshape(tp, PEER_HEADS, PEER_TOPK * PEER_TOPK), pos, axis=-1)
    gates = jax.nn.softmax(best, axis=-1)

    def block(args):
        xb, idb, gb = args
        h = jax.nn.gelu(jnp.einsum('td,thkd->thk', xb, u_table[idb]), approximate=False)
        w = (gb * h).astype(xb.dtype)
        return jnp.einsum('thk,thkd->td', w, v_table[idb]).astype(xb.dtype)

    out = lax.map(block, (xt.reshape(n, PEER_BLOCK, D),
                          ids.reshape(n, PEER_BLOCK, PEER_HEADS, PEER_TOPK),
                          gates.reshape(n, PEER_BLOCK, PEER_HEADS, PEER_TOPK)))
    return out.reshape(tp, D)[:T].reshape(B, L, D)


def trunk_layer(x, conv_buf, gla_state, w_in, conv_w, conv_b, conv_ln_g, conv_ln_b,
                gla_gate_w, gla_gate_b, gla_norm_g, w_out, ln1_g, ln1_b,
                peer_w_query, peer_sub_keys, peer_u, peer_v, ln2_g, ln2_b):
    z = x @ w_in
    c_val, c_gate, q, k, v, g, f_lr = split_in(z)
    y_conv, conv_new = conv_mixer(c_val, c_gate, conv_buf, conv_w, conv_b, conv_ln_g, conv_ln_b)
    y_gla, gla_new = gla_mixer(q, k, v, g, f_lr, gla_state, gla_gate_w, gla_gate_b, gla_norm_g)
    mix = jnp.concatenate([y_conv, y_gla.astype(y_conv.dtype)], axis=-1) @ w_out
    x = layer_norm(ALPHA * x + mix, ln1_g, ln1_b)
    x = layer_norm(ALPHA * x + peer(x, peer_w_query, peer_sub_keys, peer_u, peer_v), ln2_g, ln2_b)
    return x, conv_new, gla_new


def setup_inputs(seed: int = 0) -> dict:
    key = jax.random.key(seed)
    ks = jax.random.split(key, 24)
    nrm = lambda i, shape, s: jax.random.normal(ks[i], shape, jnp.float32) * s
    dsc = D_MODEL ** -0.5
    w_in = jnp.concatenate([
        nrm(0, (DEPTH, D_MODEL, CONV_DIM), dsc * BETA),
        nrm(1, (DEPTH, D_MODEL, CONV_DIM), dsc),
        nrm(2, (DEPTH, D_MODEL, 2 * QK_DIM), dsc),
        nrm(3, (DEPTH, D_MODEL, V_DIM), dsc * BETA),
        nrm(4, (DEPTH, D_MODEL, V_DIM + GLA_RANK), dsc),
    ], axis=-1)
    return {
        'x_prompt': nrm(5, (BATCH, SEQ, D_MODEL), 1.0),
        'x_sample': nrm(6, (DEC_BATCH, DEC_SEQ, D_MODEL), 1.0),
        'state_conv': nrm(7, (DEPTH, DEC_BATCH, CONV_WIDTH - 1, CONV_DIM), 0.5),
        'state_gla': nrm(8, (DEPTH, DEC_BATCH, GLA_HEADS, GLA_DK, GLA_DV), 0.5),
        'w_in': w_in,
        'conv_w': nrm(9, (DEPTH, CONV_WIDTH, CONV_DIM), CONV_WIDTH ** -0.5),
        'conv_b': nrm(10, (DEPTH, CONV_DIM), 0.02),
        'conv_ln_g': 1.0 + nrm(11, (DEPTH, CONV_DIM), 0.02),
        'conv_ln_b': nrm(12, (DEPTH, CONV_DIM), 0.02),
        'gla_gate_w': nrm(13, (DEPTH, GLA_RANK, QK_DIM), GLA_RANK ** -0.5),
        'gla_gate_b': 1.0 + nrm(14, (DEPTH, QK_DIM), 0.1),
        'gla_norm_g': 1.0 + nrm(15, (DEPTH, V_DIM), 0.02),
        'w_out': nrm(16, (DEPTH, D_MODEL, D_MODEL), dsc * BETA),
        'ln1_g': 1.0 + nrm(17, (DEPTH, D_MODEL), 0.02),
        'ln1_b': nrm(18, (DEPTH, D_MODEL), 0.02),
        'peer_w_query': nrm(19, (DEPTH, D_MODEL, PEER_HEADS * PEER_DKEY), dsc),
        'peer_sub_keys': nrm(20, (DEPTH, PEER_HEADS, 2, PEER_NKEYS, PEER_DKEY // 2), (PEER_DKEY // 2) ** -0.5),
        'peer_u': nrm(21, (DEPTH, N_EXPERTS, D_MODEL), dsc),
        'peer_v': nrm(22, (DEPTH, N_EXPERTS, D_MODEL), BETA),
        'ln2_g': 1.0 + nrm(23, (DEPTH, D_MODEL), 0.02),
        'ln2_b': jax.random.normal(jax.random.fold_in(key, 99), (DEPTH, D_MODEL), jnp.float32) * 0.02,
    }


def reference(x_prompt, x_sample, state_conv, state_gla, w_in, conv_w, conv_b, conv_ln_g, conv_ln_b,
              gla_gate_w, gla_gate_b, gla_norm_g, w_out, ln1_g, ln1_b,
              peer_w_query, peer_sub_keys, peer_u, peer_v, ln2_g, ln2_b):
    yp, ys = x_prompt, x_sample
    conv_p, gla_p, conv_s, gla_s = [], [], [], []
    for i in range(DEPTH):
        lp = (w_in[i], conv_w[i], conv_b[i], conv_ln_g[i], conv_ln_b[i],
              gla_gate_w[i], gla_gate_b[i], gla_norm_g[i], w_out[i], ln1_g[i], ln1_b[i],
              peer_w_query[i], peer_sub_keys[i], peer_u[i], peer_v[i], ln2_g[i], ln2_b[i])
        zero_conv = jnp.zeros((x_prompt.shape[0], CONV_WIDTH - 1, CONV_DIM), x_prompt.dtype)
        zero_gla = jnp.zeros((x_prompt.shape[0], GLA_HEADS, GLA_DK, GLA_DV), state_gla.dtype)
        yp, cp, gp = trunk_layer(yp, zero_conv, zero_gla, *lp)
        ys, cs, gs = trunk_layer(ys, state_conv[i], state_gla[i], *lp)
        conv_p.append(cp)
        gla_p.append(gp)
        conv_s.append(cs)
        gla_s.append(gs)
    return (yp, ys, jnp.stack(conv_p), jnp.stack(gla_p), jnp.stack(conv_s), jnp.stack(gla_s))
```

```python
import functools

import jax
import jax.numpy as jnp
from jax import lax
from jax.experimental import pallas as pl
from jax.experimental.pallas import tpu as pltpu

F32 = jnp.float32
BF16 = jnp.bfloat16

CONV_WIDTH = 31
GLA_HEADS = 4
GLA_RANK = 16
GLA_TAU = 16.0
PEER_HEADS = 8
PEER_NKEYS = 128
PEER_TOPK = 16
LN_EPS = 1e-5

GLA_CHUNK = 64
GLA_SUB = 16
LANES = 128
HALO = 32
NOT_LISTED = 1.0e4
VMEM_LIMIT = 56 * 1024 * 1024


def _cparams(n_axes):
    return pltpu.CompilerParams(dimension_semantics=("arbitrary",) * n_axes,
                                vmem_limit_bytes=VMEM_LIMIT)


def _layer_norm(x, g, b):
    mu = jnp.mean(x, axis=-1, keepdims=True)
    xc = x - mu
    var = jnp.mean(xc * xc, axis=-1, keepdims=True)
    return xc * lax.rsqrt(var + LN_EPS) * g + b


def _sigmoid(x):
    return 1.0 / (1.0 + jnp.exp(-x))


def _log_sigmoid(x):
    return jnp.minimum(x, 0.0) - jnp.log(1.0 + jnp.exp(-jnp.abs(x)))


def _dot_nt(a, b):
    return lax.dot_general(a, b, (((1,), (1,)), ((), ())), preferred_element_type=F32)


def _dot_tn(a, b):
    return lax.dot_general(a, b, (((0,), (0,)), ((), ())), preferred_element_type=F32)


def _split3(x):
    hi = x.astype(BF16)
    r = x - hi.astype(F32)
    mid = r.astype(BF16)
    lo = (r - mid.astype(F32)).astype(BF16)
    return hi, mid, lo


def _inproj_kernel(x_ref, w_ref, wf_ref, gw_ref, gb_ref,
                   u_ref, qk_ref, v_ref, g_ref, la_ref, xb_ref, *, half):
    j = pl.program_id(1)

    @pl.when(j == 0)
    def _():
        xb_ref[...] = x_ref[...].astype(BF16)

    z = jnp.dot(xb_ref[...], w_ref[...], preferred_element_type=F32)

    @pl.when(j == 0)
    def _():
        u_ref[:, :half] = z[:, :half] * _sigmoid(z[:, half:])

    @pl.when(j == 1)
    def _():
        u_ref[:, half:] = z[:, :half] * _sigmoid(z[:, half:])

    @pl.when(j == 2)
    def _():
        qk_ref[...] = z

    @pl.when(j == 3)
    def _():
        v_ref[...] = z

    @pl.when(j == 4)
    def _():
        g_ref[...] = z
        f = jnp.dot(xb_ref[...], wf_ref[...], preferred_element_type=F32)
        pre = jnp.dot(f, gw_ref[...], preferred_element_type=F32,
                      precision=lax.Precision.HIGHEST) + gb_ref[...]
        la_ref[...] = _log_sigmoid(pre) * (1.0 / GLA_TAU)


def _inproj(x, w_main, w_f, gw_pad, gb, *, tm):
    t, d = x.shape
    nblk = w_main.shape[1] // 1024
    conv_dim = 1024
    qk2 = 1024
    v_dim = 1024
    out_shapes = (
        jax.ShapeDtypeStruct((t, conv_dim), F32),
        jax.ShapeDtypeStruct((t, qk2), F32),
        jax.ShapeDtypeStruct((t, v_dim), F32),
        jax.ShapeDtypeStruct((t, v_dim), F32),
        jax.ShapeDtypeStruct((t, qk2 // 2), F32),
    )
    row = lambda i, j: (i, 0)
    fixed = lambda i, j: (0, 0)
    return pl.pallas_call(
        functools.partial(_inproj_kernel, half=conv_dim // 2),
        grid=(t // tm, nblk),
        in_specs=[
            pl.BlockSpec((tm, d), row),
            pl.BlockSpec((d, 1024), lambda i, j: (0, j)),
            pl.BlockSpec((d, LANES), fixed),
            pl.BlockSpec((LANES, qk2 // 2), fixed),
            pl.BlockSpec((1, qk2 // 2), fixed),
        ],
        out_specs=[
            pl.BlockSpec((tm, conv_dim), row),
            pl.BlockSpec((tm, qk2), row),
            pl.BlockSpec((tm, v_dim), row),
            pl.BlockSpec((tm, v_dim), row),
            pl.BlockSpec((tm, qk2 // 2), row),
        ],
        out_shape=out_shapes,
        scratch_shapes=[pltpu.VMEM((tm, d), BF16)],
        compiler_params=_cparams(2),
        name="inproj",
    )(x, w_main, w_f, gw_pad, gb)


def _conv_kernel(buf_ref, u_ref, w_ref, cb_ref, lg_ref, lb_ref, y_ref, st_ref, ext_ref, *, tl, rs):
    l = pl.program_id(1)

    @pl.when(l == 0)
    def _():
        ext_ref[0:HALO, :] = buf_ref[0]

    @pl.when(l > 0)
    def _():
        ext_ref[0:HALO, :] = ext_ref[tl:tl + HALO, :]

    ext_ref[HALO:HALO + tl, :] = u_ref[0]
    first = HALO - (CONV_WIDTH - 1)
    for r0 in range(0, tl, rs):
        acc = ext_ref[first + r0:first + r0 + rs, :] * w_ref[0:1, :]
        for j in range(1, CONV_WIDTH):
            acc = acc + ext_ref[first + r0 + j:first + r0 + j + rs, :] * w_ref[j:j + 1, :]
        y = _layer_norm(acc + cb_ref[...], lg_ref[...], lb_ref[...])
        y_ref[0, r0:r0 + rs, :] = y * _sigmoid(y)
    st_ref[0] = ext_ref[tl + first:tl + HALO, :]


def _conv_mixer(u, buf_pad, conv_w, conv_b, ln_g, ln_b, *, tl):
    b, l, c = u.shape
    rs = min(16, tl)
    fixed = lambda i, j: (0, 0)
    return pl.pallas_call(
        functools.partial(_conv_kernel, tl=tl, rs=rs),
        grid=(b, l // tl),
        in_specs=[
            pl.BlockSpec((1, HALO, c), lambda i, j: (i, 0, 0)),
            pl.BlockSpec((1, tl, c), lambda i, j: (i, j, 0)),
            pl.BlockSpec((CONV_WIDTH, c), fixed),
            pl.BlockSpec((1, c), fixed),
            pl.BlockSpec((1, c), fixed),
            pl.BlockSpec((1, c), fixed),
        ],
        out_specs=[
            pl.BlockSpec((1, tl, c), lambda i, j: (i, j, 0)),
            pl.BlockSpec((1, CONV_WIDTH - 1, c), lambda i, j: (i, 0, 0)),
        ],
        out_shape=(jax.ShapeDtypeStruct((b, l, c), F32),
                   jax.ShapeDtypeStruct((b, CONV_WIDTH - 1, c), F32)),
        scratch_shapes=[pltpu.VMEM((HALO + tl, c), F32)],
        compiler_params=_cparams(2),
        name="conv_mixer",
    )(buf_pad, u, conv_w, conv_b, ln_g, ln_b)


def _gla_kernel(qk_ref, v_ref, g_ref, la_ref, s0_ref, ng_ref, y_ref, sout_ref, st_ref, *, c, dk, dv):
    l = pl.program_id(1)

    @pl.when(l == 0)
    def _():
        st_ref[...] = s0_ref[0]

    hk = GLA_HEADS * dk
    la = la_ref[0]
    rows = lax.broadcasted_iota(jnp.int32, (c, c), 0)
    cols = lax.broadcasted_iota(jnp.int32, (c, c), 1)
    causal = rows >= cols
    tril = jnp.where(causal, 1.0, 0.0).astype(BF16)
    hi, mid, lo = _split3(la)
    bcum = (jnp.dot(tril, hi, preferred_element_type=F32)
            + jnp.dot(tril, mid, preferred_element_type=F32)
            + jnp.dot(tril, lo, preferred_element_type=F32))
    b_last = bcum[c - 1:c, :]
    q = qk_ref[0, :, :hk] * (dk ** -0.5)
    k = qk_ref[0, :, hk:]
    q_in = (q * jnp.exp(bcum)).astype(BF16)
    k_out = (k * jnp.exp(b_last - bcum)).astype(BF16)
    row_id = lax.broadcasted_iota(jnp.int32, (c, dk), 0)

    for h in range(GLA_HEADS):
        ks = slice(h * dk, (h + 1) * dk)
        vs = slice(h * dv, (h + 1) * dv)
        bh = bcum[:, ks]
        qh = q[:, ks]
        kh = k[:, ks]
        vh = v_ref[0, :, vs].astype(BF16)
        st = st_ref[h]
        o = _dot_nt(q_in[:, ks], st.astype(BF16))
        blocks = []
        for i in range(c // GLA_SUB):
            r0 = i * GLA_SUB
            base = bh[r0 - 1:r0, :] if i > 0 else jnp.zeros((1, dk), F32)
            q_t = qh[r0:r0 + GLA_SUB, :] * jnp.exp(bh[r0:r0 + GLA_SUB, :] - base)
            k_t = kh * jnp.exp(jnp.where(row_id < r0 + GLA_SUB, base - bh, -1e30))
            blocks.append(_dot_nt(q_t.astype(BF16), k_t.astype(BF16)))
        scores = jnp.where(causal, jnp.concatenate(blocks, axis=0), 0.0)
        o = o + jnp.dot(scores.astype(BF16), vh, preferred_element_type=F32)
        st_ref[h] = st * jnp.exp(b_last[:, ks]) + _dot_tn(vh, k_out[:, ks])
        o = o * lax.rsqrt(jnp.mean(o * o, axis=-1, keepdims=True) + LN_EPS)
        gh = g_ref[0, :, vs]
        y_ref[0, :, vs] = o * ng_ref[:, vs] * (gh * _sigmoid(gh))

    sout_ref[0] = st_ref[...]


def _gla_mixer(qk, v, g, la, s0_t, norm_g, *, c):
    b, l, hk2 = qk.shape
    dk = hk2 // (2 * GLA_HEADS)
    dv = v.shape[-1] // GLA_HEADS
    tile = lambda i, j: (i, j, 0)
    return pl.pallas_call(
        functools.partial(_gla_kernel, c=c, dk=dk, dv=dv),
        grid=(b, l // c),
        in_specs=[
            pl.BlockSpec((1, c, hk2), tile),
            pl.BlockSpec((1, c, GLA_HEADS * dv), tile),
            pl.BlockSpec((1, c, GLA_HEADS * dv), tile),
            pl.BlockSpec((1, c, hk2 // 2), tile),
            pl.BlockSpec((1, GLA_HEADS, dv, dk), lambda i, j: (i, 0, 0, 0)),
            pl.BlockSpec((1, GLA_HEADS * dv), lambda i, j: (0, 0)),
        ],
        out_specs=[
            pl.BlockSpec((1, c, GLA_HEADS * dv), tile),
            pl.BlockSpec((1, GLA_HEADS, dv, dk), lambda i, j: (i, 0, 0, 0)),
        ],
        out_shape=(jax.ShapeDtypeStruct((b, l, GLA_HEADS * dv), F32),
                   jax.ShapeDtypeStruct((b, GLA_HEADS, dv, dk), F32)),
        scratch_shapes=[pltpu.VMEM((GLA_HEADS, dv, dk), F32)],
        compiler_params=_cparams(2),
        name="gla_mixer",
    )(qk, v, g, la, s0_t, norm_g)


def _outproj_kernel(x_ref, yc_ref, yg_ref, wc_ref, wg_ref, g_ref, b_ref, o_ref, *, alpha):
    mix = (jnp.dot(yc_ref[...].astype(BF16), wc_ref[...], preferred_element_type=F32)
           + jnp.dot(yg_ref[...].astype(BF16), wg_ref[...], preferred_element_type=F32))
    o_ref[...] = _layer_norm(alpha * x_ref[...] + mix, g_ref[...], b_ref[...])


def _outproj(x, y_conv, y_gla, w_c, w_g, ln_g, ln_b, *, tm, alpha):
    t, d = x.shape
    row = lambda i: (i, 0)
    fixed = lambda i: (0, 0)
    return pl.pallas_call(
        functools.partial(_outproj_kernel, alpha=alpha),
        grid=(t // tm,),
        in_specs=[
            pl.BlockSpec((tm, d), row),
            pl.BlockSpec((tm, y_conv.shape[1]), row),
            pl.BlockSpec((tm, y_gla.shape[1]), row),
            pl.BlockSpec(w_c.shape, fixed),
            pl.BlockSpec(w_g.shape, fixed),
            pl.BlockSpec((1, d), fixed),
            pl.BlockSpec((1, d), fixed),
        ],
        out_specs=pl.BlockSpec((tm, d), row),
        out_shape=jax.ShapeDtypeStruct((t, d), F32),
        compiler_params=_cparams(1),
        name="outproj_ln",
    )(x, y_conv, y_gla, w_c, w_g, ln_g, ln_b)


def _extract_topk(s):
    n = s.shape[0]
    rows = lax.broadcasted_iota(jnp.int32, s.shape, 0)
    rank = jnp.full(s.shape, NOT_LISTED, F32)
    vals = []
    for r in range(PEER_TOPK):
        m = jnp.max(s, axis=0, keepdims=True)
        first = jnp.min(jnp.where(s == m, rows, n), axis=0, keepdims=True)
        hit = rows == first
        rank = jnp.where(hit, float(r), rank)
        s = jnp.where(hit, -jnp.inf, s)
        vals.append(m)
    return vals, rank


def _route_kernel(x_ref, wq_ref, keys_ref, na_ref, bm_ref, e1_ref, e2_ref):
    k = PEER_TOPK
    xq = jnp.dot(x_ref[...].astype(BF16), wq_ref[...], preferred_element_type=F32)
    dsub = keys_ref.shape[-1]
    for h in range(PEER_HEADS):
        s_t = []
        for p in range(2):
            off = (h * 2 + p) * dsub
            qh, qm, ql = _split3(xq[:, off:off + dsub])
            kh, km, kl = _split3(keys_ref[h * 2 + p])
            s_t.append(_dot_nt(kh, qh) + (_dot_nt(kh, qm) + _dot_nt(km, qh))
                       + (_dot_nt(kh, ql) + _dot_nt(km, qm) + _dot_nt(kl, qh)))
        vals1, rank1 = _extract_topk(s_t[0])
        vals2, rank2 = _extract_topk(s_t[1])
        v2 = jnp.concatenate(vals2, axis=0)
        cand = jnp.concatenate([vals1[a] + v2 for a in range(k)], axis=0)
        ev1 = [jnp.exp(vals1[a] - vals1[0]) for a in range(k)]
        ev2 = jnp.exp(v2 - vals2[0])
        gate = jnp.concatenate([ev1[a] * ev2 for a in range(k)], axis=0)
        rows = lax.broadcasted_iota(jnp.int32, cand.shape, 0)
        sel = jnp.zeros(cand.shape, F32)
        for _ in range(k):
            m = jnp.max(cand, axis=0, keepdims=True)
            first = jnp.min(jnp.where(cand == m, rows, k * k), axis=0, keepdims=True)
            hit = rows == first
            sel = jnp.where(hit, 1.0, sel)
            cand = jnp.where(hit, -jnp.inf, cand)
        z = jnp.sum(sel * gate, axis=0, keepdims=True)
        na = jnp.full(rank1.shape, NOT_LISTED, F32)
        for a in range(k):
            height = jnp.sum(sel[a * k:(a + 1) * k, :], axis=0, keepdims=True)
            na = jnp.where(rank1 == float(a), 1.0 - height, na)
        na_ref[h] = na
        bm_ref[h] = jnp.where(rank2 < float(k), -rank2, -2.0 * NOT_LISTED)
        e1_ref[h] = jnp.exp(s_t[0] - vals1[0]) / z
        e2_ref[h] = jnp.exp(s_t[1] - vals2[0])


def _route(x1, wq, keys, *, tm):
    t, d = x1.shape
    nk = keys.shape[1]
    out = jax.ShapeDtypeStruct((PEER_HEADS, nk, t), F32)
    ospec = pl.BlockSpec((PEER_HEADS, nk, tm), lambda i: (0, 0, i))
    return pl.pallas_call(
        _route_kernel,
        grid=(t // tm,),
        in_specs=[
            pl.BlockSpec((tm, d), lambda i: (i, 0)),
            pl.BlockSpec(wq.shape, lambda i: (0, 0)),
            pl.BlockSpec(keys.shape, lambda i: (0, 0, 0)),
        ],
        out_specs=[ospec] * 4,
        out_shape=(out,) * 4,
        compiler_params=_cparams(1),
        name="peer_route",
    )(x1, wq, keys)


def _gelu(x):
    return 0.5 * x * (1.0 + lax.erf(x * 0.7071067811865476))


def _peer_kernel(x_ref, na_ref, bm_ref, e1_ref, e2_ref, u_ref, vt_ref, g_ref, b_ref,
                 y_ref, xt_ref, p_ref, acc_ref, *, ib, alpha):
    step = pl.program_id(1)

    @pl.when(step == 0)
    def _():
        xt_ref[...] = x_ref[...].T.astype(BF16)
        acc_ref[...] = jnp.zeros_like(acc_ref)

    nk = bm_ref.shape[1]
    h_t = jnp.dot(u_ref[...], xt_ref[...], preferred_element_type=F32)
    for ii in range(ib):
        w = jnp.zeros((nk, xt_ref.shape[1]), F32)
        for h in range(PEER_HEADS):
            picked = bm_ref[h] >= na_ref[h, ii:ii + 1, :]
            w = w + jnp.where(picked, e2_ref[h] * e1_ref[h, ii:ii + 1, :], 0.0)
        p_ref[ii * nk:(ii + 1) * nk, :] = (_gelu(h_t[ii * nk:(ii + 1) * nk, :]) * w).astype(BF16)
    acc_ref[...] += jnp.dot(vt_ref[...], p_ref[...], preferred_element_type=F32)

    @pl.when(step == pl.num_programs(1) - 1)
    def _():
        y_ref[...] = _layer_norm(alpha * x_ref[...] + acc_ref[...].T, g_ref[...], b_ref[...])


def _peer_dense(x1, na, bm, e1, e2, u_bf, vt_bf, ln_g, ln_b, *, tn, ib, alpha):
    t, d = x1.shape
    nk = bm.shape[1]
    ne = u_bf.shape[0]
    row_code = pl.BlockSpec((PEER_HEADS, ib, tn), lambda i, s: (0, s, i))
    col_code = pl.BlockSpec((PEER_HEADS, nk, tn), lambda i, s: (0, 0, i))
    return pl.pallas_call(
        functools.partial(_peer_kernel, ib=ib, alpha=alpha),
        grid=(t // tn, ne // (ib * nk)),
        in_specs=[
            pl.BlockSpec((tn, d), lambda i, s: (i, 0)),
            row_code, col_code, row_code, col_code,
            pl.BlockSpec((ib * nk, d), lambda i, s: (s, 0)),
            pl.BlockSpec((d, ib * nk), lambda i, s: (0, s)),
            pl.BlockSpec((1, d), lambda i, s: (0, 0)),
            pl.BlockSpec((1, d), lambda i, s: (0, 0)),
        ],
        out_specs=pl.BlockSpec((tn, d), lambda i, s: (i, 0)),
        out_shape=jax.ShapeDtypeStruct((t, d), F32),
        scratch_shapes=[pltpu.VMEM((d, tn), BF16),
                        pltpu.VMEM((ib * nk, tn), BF16),
                        pltpu.VMEM((d, tn), F32)],
        compiler_params=_cparams(2),
        name="peer_dense",
    )(x1, na, bm, e1, e2, u_bf, vt_bf, ln_g, ln_b)


def _pick(n, pref):
    return pref if n % pref == 0 else n


def _trunk_layer(x, conv_buf, gla_state, prm, alpha):
    b, l, d = x.shape
    t = b * l
    xf = x.reshape(t, d)
    tm = _pick(t, 256)

    u, qk, v, g, la = _inproj(xf, prm["w_main"], prm["w_f"], prm["gw_pad"], prm["gate_b"], tm=tm)
    conv_dim = u.shape[-1]

    buf_pad = jnp.pad(conv_buf, ((0, 0), (HALO - (CONV_WIDTH - 1), 0), (0, 0)))
    y_conv, conv_new = _conv_mixer(u.reshape(b, l, conv_dim), buf_pad, prm["conv_w"], prm["conv_b"],
                                   prm["conv_ln_g"], prm["conv_ln_b"], tl=_pick(l, 256))

    c = _pick(l, GLA_CHUNK)
    s0_t = jnp.swapaxes(gla_state, -1, -2)
    y_gla, s_t = _gla_mixer(qk.reshape(b, l, -1), v.reshape(b, l, -1), g.reshape(b, l, -1),
                            la.reshape(b, l, -1), s0_t, prm["gla_norm_g"], c=c)
    gla_new = jnp.swapaxes(s_t, -1, -2)

    x1 = _outproj(xf, y_conv.reshape(t, -1), y_gla.reshape(t, -1), prm["w_out_c"], prm["w_out_g"],
                  prm["ln1_g"], prm["ln1_b"], tm=tm, alpha=alpha)

    na, bm, e1, e2 = _route(x1, prm["wq"], prm["keys"], tm=tm)
    y = _peer_dense(x1, na, bm, e1, e2, prm["u_bf"], prm["vt_bf"], prm["ln2_g"], prm["ln2_b"],
                    tn=tm, ib=8, alpha=alpha)
    return y.reshape(b, l, d), conv_new, gla_new


def _prep_layer(i, w_in, conv_w, conv_b, conv_ln_g, conv_ln_b, gla_gate_w, gla_gate_b, gla_norm_g,
                w_out, ln1_g, ln1_b, peer_w_query, peer_sub_keys, peer_u, peer_v, ln2_g, ln2_b):
    conv_dim = conv_w.shape[-1]
    half = conv_dim // 2
    wi = w_in[i]
    a_w, g_w, rest = wi[:, :conv_dim], wi[:, conv_dim:2 * conv_dim], wi[:, 2 * conv_dim:]
    n_rest = rest.shape[1] - GLA_RANK
    w_main = jnp.concatenate([a_w[:, :half], g_w[:, :half], a_w[:, half:], g_w[:, half:],
                              rest[:, :n_rest]], axis=1).astype(BF16)
    w_f = jnp.pad(rest[:, n_rest:], ((0, 0), (0, LANES - GLA_RANK))).astype(BF16)
    gw_pad = jnp.pad(gla_gate_w[i], ((0, LANES - GLA_RANK), (0, 0)))
    row = lambda a: a[i][None, :]
    keys = peer_sub_keys[i]
    return dict(
        w_main=w_main, w_f=w_f, gw_pad=gw_pad, gate_b=row(gla_gate_b),
        conv_w=conv_w[i], conv_b=row(conv_b), conv_ln_g=row(conv_ln_g), conv_ln_b=row(conv_ln_b),
        gla_norm_g=row(gla_norm_g),
        w_out_c=w_out[i][:conv_dim].astype(BF16), w_out_g=w_out[i][conv_dim:].astype(BF16),
        ln1_g=row(ln1_g), ln1_b=row(ln1_b),
        wq=peer_w_query[i].astype(BF16),
        keys=keys.reshape(keys.shape[0] * 2, keys.shape[2], keys.shape[3]),
        u_bf=peer_u[i].astype(BF16), vt_bf=peer_v[i].T.astype(BF16),
        ln2_g=row(ln2_g), ln2_b=row(ln2_b),
    )


def kernel(x_prompt, x_sample, state_conv, state_gla, w_in, conv_w, conv_b, conv_ln_g, conv_ln_b,
           gla_gate_w, gla_gate_b, gla_norm_g, w_out, ln1_g, ln1_b,
           peer_w_query, peer_sub_keys, peer_u, peer_v, ln2_g, ln2_b):
    depth = w_in.shape[0]
    alpha = (2.0 * depth) ** 0.25
    yp, ys = x_prompt, x_sample
    conv_p, gla_p, conv_s, gla_s = [], [], [], []
    for i in range(depth):
        prm = _prep_layer(i, w_in, conv_w, conv_b, conv_ln_g, conv_ln_b, gla_gate_w, gla_gate_b,
                          gla_norm_g, w_out, ln1_g, ln1_b, peer_w_query, peer_sub_keys,
                          peer_u, peer_v, ln2_g, ln2_b)
        bp = x_prompt.shape[0]
        zero_conv = jnp.zeros((bp,) + state_conv.shape[2:], x_prompt.dtype)
        zero_gla = jnp.zeros((bp,) + state_gla.shape[2:], state_gla.dtype)
        yp, cp, gp = _trunk_layer(yp, zero_conv, zero_gla, prm, alpha)
        ys, cs, gs = _trunk_layer(ys, state_conv[i], state_gla[i], prm, alpha)
        conv_p.append(cp)
        gla_p.append(gp)
        conv_s.append(cs)
        gla_s.append(gs)
    return (yp, ys, jnp.stack(conv_p), jnp.stack(gla_p), jnp.stack(conv_s), jnp.stack(gla_s))
```

```python
import functools

import jax
import jax.numpy as jnp
from jax import lax
from jax.experimental import pallas as pl
from jax.experimental.pallas import tpu as pltpu

F32 = jnp.float32
BF16 = jnp.bfloat16

CONV_WIDTH = 31
GLA_HEADS = 4
GLA_RANK = 16
GLA_TAU = 16.0
PEER_HEADS = 8
PEER_NKEYS = 128
PEER_TOPK = 16
LN_EPS = 1e-5

GLA_CHUNK = 64
GLA_SUB = 16
LANES = 128
HALO = 32
NOT_LISTED = 8192.0
VMEM_LIMIT = 56 * 1024 * 1024


def _cparams(n_axes):
    return pltpu.CompilerParams(dimension_semantics=("arbitrary",) * n_axes,
                                vmem_limit_bytes=VMEM_LIMIT)


def _layer_norm(x, g, b):
    mu = jnp.mean(x, axis=-1, keepdims=True)
    xc = x - mu
    var = jnp.mean(xc * xc, axis=-1, keepdims=True)
    return xc * lax.rsqrt(var + LN_EPS) * g + b


def _sigmoid(x):
    return 1.0 / (1.0 + jnp.exp(-x))


def _log_sigmoid(x):
    return jnp.minimum(x, 0.0) - jnp.log(1.0 + jnp.exp(-jnp.abs(x)))


def _dot_nt(a, b):
    return lax.dot_general(a, b, (((1,), (1,)), ((), ())), preferred_element_type=F32)


def _dot_tn(a, b):
    return lax.dot_general(a, b, (((0,), (0,)), ((), ())), preferred_element_type=F32)


def _split3(x):
    hi = x.astype(BF16)
    r = x - hi.astype(F32)
    mid = r.astype(BF16)
    lo = (r - mid.astype(F32)).astype(BF16)
    return hi, mid, lo


def _inproj_kernel(x_ref, w_ref, wf_ref, gw_ref, gb_ref,
                   u_ref, qk_ref, v_ref, g_ref, la_ref, xb_ref, *, half):
    j = pl.program_id(1)

    @pl.when(j == 0)
    def _():
        xb_ref[...] = x_ref[...].astype(BF16)

    z = jnp.dot(xb_ref[...], w_ref[...], preferred_element_type=F32)

    @pl.when(j == 0)
    def _():
        u_ref[:, :half] = z[:, :half] * _sigmoid(z[:, half:])

    @pl.when(j == 1)
    def _():
        u_ref[:, half:] = z[:, :half] * _sigmoid(z[:, half:])

    @pl.when(j == 2)
    def _():
        qk_ref[...] = z

    @pl.when(j == 3)
    def _():
        v_ref[...] = z

    @pl.when(j == 4)
    def _():
        g_ref[...] = z
        f = jnp.dot(xb_ref[...], wf_ref[...], preferred_element_type=F32)
        pre = jnp.dot(f, gw_ref[...], preferred_element_type=F32,
                      precision=lax.Precision.HIGHEST) + gb_ref[...]
        la_ref[...] = _log_sigmoid(pre) * (1.0 / GLA_TAU)


def _inproj(x, w_main, w_f, gw_pad, gb, *, tm):
    t, d = x.shape
    nblk = w_main.shape[1] // 1024
    conv_dim = 1024
    qk2 = 1024
    v_dim = 1024
    out_shapes = (
        jax.ShapeDtypeStruct((t, conv_dim), F32),
        jax.ShapeDtypeStruct((t, qk2), F32),
        jax.ShapeDtypeStruct((t, v_dim), F32),
        jax.ShapeDtypeStruct((t, v_dim), F32),
        jax.ShapeDtypeStruct((t, qk2 // 2), F32),
    )
    row = lambda i, j: (i, 0)
    fixed = lambda i, j: (0, 0)
    return pl.pallas_call(
        functools.partial(_inproj_kernel, half=conv_dim // 2),
        grid=(t // tm, nblk),
        in_specs=[
            pl.BlockSpec((tm, d), row),
            pl.BlockSpec((d, 1024), lambda i, j: (0, j)),
            pl.BlockSpec((d, LANES), fixed),
            pl.BlockSpec((LANES, qk2 // 2), fixed),
            pl.BlockSpec((1, qk2 // 2), fixed),
        ],
        out_specs=[
            pl.BlockSpec((tm, conv_dim), row),
            pl.BlockSpec((tm, qk2), row),
            pl.BlockSpec((tm, v_dim), row),
            pl.BlockSpec((tm, v_dim), row),
            pl.BlockSpec((tm, qk2 // 2), row),
        ],
        out_shape=out_shapes,
        scratch_shapes=[pltpu.VMEM((tm, d), BF16)],
        compiler_params=_cparams(2),
        name="inproj",
    )(x, w_main, w_f, gw_pad, gb)


def _conv_kernel(buf_ref, u_ref, w_ref, cb_ref, lg_ref, lb_ref, y_ref, st_ref, ext_ref, *, tl, rs):
    l = pl.program_id(1)

    @pl.when(l == 0)
    def _():
        ext_ref[0:HALO, :] = buf_ref[0]

    @pl.when(l > 0)
    def _():
        ext_ref[0:HALO, :] = ext_ref[tl:tl + HALO, :]

    ext_ref[HALO:HALO + tl, :] = u_ref[0]
    first = HALO - (CONV_WIDTH - 1)
    for r0 in range(0, tl, rs):
        acc = ext_ref[first + r0:first + r0 + rs, :] * w_ref[0:1, :]
        for j in range(1, CONV_WIDTH):
            acc = acc + ext_ref[first + r0 + j:first + r0 + j + rs, :] * w_ref[j:j + 1, :]
        y = _layer_norm(acc + cb_ref[...], lg_ref[...], lb_ref[...])
        y_ref[0, r0:r0 + rs, :] = y * _sigmoid(y)
    st_ref[0] = ext_ref[tl + first:tl + HALO, :]


def _conv_mixer(u, buf_pad, conv_w, conv_b, ln_g, ln_b, *, tl):
    b, l, c = u.shape
    rs = min(16, tl)
    fixed = lambda i, j: (0, 0)
    return pl.pallas_call(
        functools.partial(_conv_kernel, tl=tl, rs=rs),
        grid=(b, l // tl),
        in_specs=[
            pl.BlockSpec((1, HALO, c), lambda i, j: (i, 0, 0)),
            pl.BlockSpec((1, tl, c), lambda i, j: (i, j, 0)),
            pl.BlockSpec((CONV_WIDTH, c), fixed),
            pl.BlockSpec((1, c), fixed),
            pl.BlockSpec((1, c), fixed),
            pl.BlockSpec((1, c), fixed),
        ],
        out_specs=[
            pl.BlockSpec((1, tl, c), lambda i, j: (i, j, 0)),
            pl.BlockSpec((1, CONV_WIDTH - 1, c), lambda i, j: (i, 0, 0)),
        ],
        out_shape=(jax.ShapeDtypeStruct((b, l, c), F32),
                   jax.ShapeDtypeStruct((b, CONV_WIDTH - 1, c), F32)),
        scratch_shapes=[pltpu.VMEM((HALO + tl, c), F32)],
        compiler_params=_cparams(2),
        name="conv_mixer",
    )(buf_pad, u, conv_w, conv_b, ln_g, ln_b)


def _gla_kernel(qk_ref, v_ref, g_ref, la_ref, s0_ref, ng_ref, y_ref, sout_ref, st_ref, *, c, dk, dv):
    l = pl.program_id(1)

    @pl.when(l == 0)
    def _():
        st_ref[...] = s0_ref[0]

    hk = GLA_HEADS * dk
    la = la_ref[0]
    rows = lax.broadcasted_iota(jnp.int32, (c, c), 0)
    cols = lax.broadcasted_iota(jnp.int32, (c, c), 1)
    causal = rows >= cols
    tril = jnp.where(causal, 1.0, 0.0).astype(BF16)
    hi, mid, lo = _split3(la)
    bcum = (jnp.dot(tril, hi, preferred_element_type=F32)
            + jnp.dot(tril, mid, preferred_element_type=F32)
            + jnp.dot(tril, lo, preferred_element_type=F32))
    b_last = bcum[c - 1:c, :]
    q = qk_ref[0, :, :hk] * (dk ** -0.5)
    k = qk_ref[0, :, hk:]
    q_in = (q * jnp.exp(bcum)).astype(BF16)
    k_out = (k * jnp.exp(b_last - bcum)).astype(BF16)
    row_id = lax.broadcasted_iota(jnp.int32, (c, dk), 0)

    for h in range(GLA_HEADS):
        ks = slice(h * dk, (h + 1) * dk)
        vs = slice(h * dv, (h + 1) * dv)
        bh = bcum[:, ks]
        qh = q[:, ks]
        kh = k[:, ks]
        vh = v_ref[0, :, vs].astype(BF16)
        st = st_ref[h]
        o = _dot_nt(q_in[:, ks], st.astype(BF16))
        blocks = []
        for i in range(c // GLA_SUB):
            r0 = i * GLA_SUB
            base = bh[r0 - 1:r0, :] if i > 0 else jnp.zeros((1, dk), F32)
            q_t = qh[r0:r0 + GLA_SUB, :] * jnp.exp(bh[r0:r0 + GLA_SUB, :] - base)
            k_t = kh * jnp.exp(jnp.where(row_id < r0 + GLA_SUB, base - bh, -1e30))
            blocks.append(_dot_nt(q_t.astype(BF16), k_t.astype(BF16)))
        scores = jnp.where(causal, jnp.concatenate(blocks, axis=0), 0.0)
        o = o + jnp.dot(scores.astype(BF16), vh, preferred_element_type=F32)
        st_ref[h] = st * jnp.exp(b_last[:, ks]) + _dot_tn(vh, k_out[:, ks])
        o = o * lax.rsqrt(jnp.mean(o * o, axis=-1, keepdims=True) + LN_EPS)
        gh = g_ref[0, :, vs]
        y_ref[0, :, vs] = o * ng_ref[:, vs] * (gh * _sigmoid(gh))

    sout_ref[0] = st_ref[...]


def _gla_mixer(qk, v, g, la, s0_t, norm_g, *, c):
    b, l, hk2 = qk.shape
    dk = hk2 // (2 * GLA_HEADS)
    dv = v.shape[-1] // GLA_HEADS
    tile = lambda i, j: (i, j, 0)
    return pl.pallas_call(
        functools.partial(_gla_kernel, c=c, dk=dk, dv=dv),
        grid=(b, l // c),
        in_specs=[
            pl.BlockSpec((1, c, hk2), tile),
            pl.BlockSpec((1, c, GLA_HEADS * dv), tile),
            pl.BlockSpec((1, c, GLA_HEADS * dv), tile),
            pl.BlockSpec((1, c, hk2 // 2), tile),
            pl.BlockSpec((1, GLA_HEADS, dv, dk), lambda i, j: (i, 0, 0, 0)),
            pl.BlockSpec((1, GLA_HEADS * dv), lambda i, j: (0, 0)),
        ],
        out_specs=[
            pl.BlockSpec((1, c, GLA_HEADS * dv), tile),
            pl.BlockSpec((1, GLA_HEADS, dv, dk), lambda i, j: (i, 0, 0, 0)),
        ],
        out_shape=(jax.ShapeDtypeStruct((b, l, GLA_HEADS * dv), F32),
                   jax.ShapeDtypeStruct((b, GLA_HEADS, dv, dk), F32)),
        scratch_shapes=[pltpu.VMEM((GLA_HEADS, dv, dk), F32)],
        compiler_params=_cparams(2),
        name="gla_mixer",
    )(qk, v, g, la, s0_t, norm_g)


def _outproj_kernel(x_ref, yc_ref, yg_ref, wc_ref, wg_ref, g_ref, b_ref, o_ref, *, alpha):
    mix = (jnp.dot(yc_ref[...].astype(BF16), wc_ref[...], preferred_element_type=F32)
           + jnp.dot(yg_ref[...].astype(BF16), wg_ref[...], preferred_element_type=F32))
    o_ref[...] = _layer_norm(alpha * x_ref[...] + mix, g_ref[...], b_ref[...])


def _outproj(x, y_conv, y_gla, w_c, w_g, ln_g, ln_b, *, tm, alpha):
    t, d = x.shape
    row = lambda i: (i, 0)
    fixed = lambda i: (0, 0)
    return pl.pallas_call(
        functools.partial(_outproj_kernel, alpha=alpha),
        grid=(t // tm,),
        in_specs=[
            pl.BlockSpec((tm, d), row),
            pl.BlockSpec((tm, y_conv.shape[1]), row),
            pl.BlockSpec((tm, y_gla.shape[1]), row),
            pl.BlockSpec(w_c.shape, fixed),
            pl.BlockSpec(w_g.shape, fixed),
            pl.BlockSpec((1, d), fixed),
            pl.BlockSpec((1, d), fixed),
        ],
        out_specs=pl.BlockSpec((tm, d), row),
        out_shape=jax.ShapeDtypeStruct((t, d), F32),
        compiler_params=_cparams(1),
        name="outproj_ln",
    )(x, y_conv, y_gla, w_c, w_g, ln_g, ln_b)


def _extract_topk(s, exact):
    n = s.shape[0]
    rows = lax.broadcasted_iota(jnp.int32, s.shape, 0).astype(F32) if exact else None
    rank = jnp.full(s.shape, NOT_LISTED, F32)
    vals = []
    for r in range(PEER_TOPK):
        m = jnp.max(s, axis=0, keepdims=True)
        if exact:
            first = jnp.min(jnp.where(s == m, rows, float(n)), axis=0, keepdims=True)
            hit = rows == first
        else:
            hit = s == m
        rank = jnp.where(hit, float(r), rank)
        s = jnp.where(hit, -jnp.inf, s)
        vals.append(m)
    return vals, rank


def _dup16(x):
    bits = lax.bitcast_convert_type(x.astype(BF16).astype(F32), jnp.uint32)
    return lax.bitcast_convert_type(bits | (bits >> 16), jnp.int32)


def _route_head(s1, s2, exact):
    k = PEER_TOPK
    assert k == 16
    vals1, rank1 = _extract_topk(s1, exact)
    vals2, rank2 = _extract_topk(s2, exact)
    v2 = jnp.concatenate(vals2, axis=0)
    v1_tail = jnp.concatenate(vals1[8:], axis=0)
    ev2 = jnp.exp(v2 - vals2[0])
    ev1 = [jnp.exp(vals1[a] - vals1[0]) for a in range(8)]
    ev1_tail = jnp.exp(v1_tail - vals1[0])
    cand = jnp.concatenate([vals1[0] + v2] + [vals1[a] + v2[0:8] for a in range(1, 8)]
                           + [v1_tail + vals2[0]], axis=0)
    gate = jnp.concatenate([ev1[0] * ev2] + [ev1[a] * ev2[0:8] for a in range(1, 8)]
                           + [ev1_tail * ev2[0:1]], axis=0)
    r = lax.broadcasted_iota(jnp.int32, cand.shape, 0)
    a_mid = 1 + ((r - 16) >> 3)
    b_mid = (r - 16) & 7
    head_rows = r < 16
    tail_rows = r >= 72
    flat = jnp.where(head_rows, r, jnp.where(tail_rows, (r - 64) * k, a_mid * k + b_mid)).astype(F32)
    valid = head_rows | tail_rows | ((a_mid + 1) * (b_mid + 1) <= k)
    cand = jnp.where(valid, cand, -jnp.inf)
    sel = jnp.zeros(cand.shape, F32)
    for _ in range(k):
        m = jnp.max(cand, axis=0, keepdims=True)
        if exact:
            first = jnp.min(jnp.where(cand == m, flat, float(k * k)), axis=0, keepdims=True)
            hit = flat == first
        else:
            hit = cand == m
        sel = jnp.where(hit, 1.0, sel)
        cand = jnp.where(hit, -jnp.inf, cand)
    z = jnp.sum(sel * gate, axis=0, keepdims=True)
    heights = [jnp.sum(sel[0:16], axis=0, keepdims=True)]
    heights += [jnp.sum(sel[8 + 8 * a:16 + 8 * a], axis=0, keepdims=True) for a in range(1, 8)]
    heights += [sel[64 + a:65 + a] for a in range(8, 16)]
    na = jnp.full(rank1.shape, NOT_LISTED, F32)
    for a in range(k):
        na = jnp.where(rank1 == float(a), 1.0 - heights[a], na)
    bm = jnp.where(rank2 < float(k), -rank2, -2.0 * NOT_LISTED)
    e1 = jnp.exp(s1 - vals1[0]) / z
    e2 = jnp.exp(s2 - vals2[0])
    count = lambda rank: jnp.sum(jnp.where(rank < NOT_LISTED, 1.0, 0.0), axis=0, keepdims=True)
    overfull = ((count(rank1) != float(k)) | (count(rank2) != float(k))
                | (jnp.sum(sel, axis=0, keepdims=True) != float(k)))
    return na, bm, e1, e2, overfull


def _route_kernel(x_ref, wq_ref, keys_ref, na_ref, bm_ref, e1_ref, e2_ref, s_ref):
    xq = jnp.dot(x_ref[...].astype(BF16), wq_ref[...], preferred_element_type=F32)
    dsub = keys_ref.shape[-1]
    for h in range(PEER_HEADS):
        for p in range(2):
            off = (h * 2 + p) * dsub
            qh, qm, ql = _split3(xq[:, off:off + dsub])
            kh, km, kl = _split3(keys_ref[h * 2 + p])
            s_ref[p] = (_dot_nt(kh, qh) + (_dot_nt(kh, qm) + _dot_nt(km, qh))
                        + (_dot_nt(kh, ql) + _dot_nt(km, qm) + _dot_nt(kl, qh)))

        def emit(exact, h=h):
            na, bm, e1, e2, overfull = _route_head(s_ref[0], s_ref[1], exact)
            na_ref[h] = _dup16(na)
            bm_ref[h] = bm.astype(BF16)
            e1_ref[h] = _dup16(e1)
            e2_ref[h] = e2.astype(BF16)
            return overfull

        overfull = emit(False)
        n_bad = jnp.sum(jnp.where(overfull, 1.0, 0.0))

        @pl.when(n_bad > 0.0)
        def _():
            emit(True)


def _route(x1, wq, keys, *, tm):
    t, d = x1.shape
    nk = keys.shape[1]
    ospec = pl.BlockSpec((PEER_HEADS, nk, tm), lambda i: (0, 0, i))
    words = jax.ShapeDtypeStruct((PEER_HEADS, nk, t), jnp.int32)
    halfs = jax.ShapeDtypeStruct((PEER_HEADS, nk, t), BF16)
    return pl.pallas_call(
        _route_kernel,
        grid=(t // tm,),
        in_specs=[
            pl.BlockSpec((tm, d), lambda i: (i, 0)),
            pl.BlockSpec(wq.shape, lambda i: (0, 0)),
            pl.BlockSpec(keys.shape, lambda i: (0, 0, 0)),
        ],
        out_specs=[ospec] * 4,
        out_shape=(words, halfs, words, halfs),
        scratch_shapes=[pltpu.VMEM((2, nk, tm), F32)],
        compiler_params=_cparams(1),
        name="peer_route",
    )(x1, wq, keys)


def _gelu(x):
    return 0.5 * x * (1.0 + lax.erf(x * 0.7071067811865476))


def _peer_kernel(x_ref, na_ref, bm_ref, e1_ref, e2_ref, u_ref, vt_ref, g_ref, b_ref, y_ref,
                 xt_ref, h0_ref, h1_ref, p0_ref, p1_ref, acc_ref, *, ib, nblk, alpha):
    step = pl.program_id(1)
    nk = bm_ref.shape[1]
    tn = xt_ref.shape[1]

    def stage(h_new, h_old, p_new, p_old, project, weigh, combine):
        parts = 4
        group = 16
        hm = ib * nk // parts
        dm = acc_ref.shape[0] // parts
        for q in range(parts):
            if project:
                h_new[q * hm:(q + 1) * hm, :] = jnp.dot(u_ref[q * hm:(q + 1) * hm, :], xt_ref[...],
                                                        preferred_element_type=F32)
            if weigh:
                for ii in range(q * ib // parts, (q + 1) * ib // parts):
                    row = lambda ref, h: pltpu.bitcast(
                        jnp.broadcast_to(ref[h, ii:ii + 1, :], (group // 2, tn)), BF16)
                    na_b = [row(na_ref, h) for h in range(PEER_HEADS)]
                    e1_b = [row(e1_ref, h) for h in range(PEER_HEADS)]
                    for r0 in range(0, nk, group):
                        cols = slice(r0, r0 + group)
                        w = jnp.zeros((group, tn), BF16)
                        for h in range(PEER_HEADS):
                            w = w + jnp.where(bm_ref[h, cols, :] >= na_b[h], e2_ref[h, cols, :] * e1_b[h],
                                              jnp.zeros((group, tn), BF16))
                        rows = slice(ii * nk + r0, ii * nk + r0 + group)
                        p_new[rows, :] = _gelu(h_old[rows, :]).astype(BF16) * w
            if combine:
                acc_ref[q * dm:(q + 1) * dm, :] += jnp.dot(vt_ref[q * dm:(q + 1) * dm, :], p_old[...],
                                                           preferred_element_type=F32)

    @pl.when(step == 0)
    def _():
        xt_ref[...] = x_ref[...].T.astype(BF16)
        acc_ref[...] = jnp.zeros_like(acc_ref)
        stage(h0_ref, None, None, None, True, False, False)

    @pl.when(step == 1)
    def _():
        stage(h1_ref, h0_ref, p1_ref, None, True, True, False)

    mid = (step >= 2) & (step < nblk)

    @pl.when(mid & (step % 2 == 0))
    def _():
        stage(h0_ref, h1_ref, p0_ref, p1_ref, True, True, True)

    @pl.when(mid & (step % 2 == 1))
    def _():
        stage(h1_ref, h0_ref, p1_ref, p0_ref, True, True, True)

    @pl.when(step == nblk)
    def _():
        stage(None, h1_ref, p0_ref, p1_ref, False, True, True)

    @pl.when(step == nblk + 1)
    def _():
        stage(None, None, None, p0_ref, False, False, True)
        y_ref[...] = _layer_norm(alpha * x_ref[...] + acc_ref[...].T, g_ref[...], b_ref[...])


def _peer_dense(x1, na, bm, e1, e2, u_bf, vt_bf, ln_g, ln_b, *, tn, ib, alpha):
    t, d = x1.shape
    nk = bm.shape[1]
    nblk = u_bf.shape[0] // (ib * nk)
    assert nblk % 2 == 0 and nblk >= 4
    last = nblk - 1
    weigh_blk = lambda s: jnp.clip(s - 1, 0, last)
    row_code = pl.BlockSpec((PEER_HEADS, ib, tn), lambda i, s: (0, weigh_blk(s), i))
    col_code = pl.BlockSpec((PEER_HEADS, nk, tn), lambda i, s: (0, 0, i))
    return pl.pallas_call(
        functools.partial(_peer_kernel, ib=ib, nblk=nblk, alpha=alpha),
        grid=(t // tn, nblk + 2),
        in_specs=[
            pl.BlockSpec((tn, d), lambda i, s: (i, 0)),
            row_code, col_code, row_code, col_code,
            pl.BlockSpec((ib * nk, d), lambda i, s: (jnp.minimum(s, last), 0)),
            pl.BlockSpec((d, ib * nk), lambda i, s: (0, jnp.clip(s - 2, 0, last))),
            pl.BlockSpec((1, d), lambda i, s: (0, 0)),
            pl.BlockSpec((1, d), lambda i, s: (0, 0)),
        ],
        out_specs=pl.BlockSpec((tn, d), lambda i, s: (i, 0)),
        out_shape=jax.ShapeDtypeStruct((t, d), F32),
        scratch_shapes=[pltpu.VMEM((d, tn), BF16),
                        pltpu.VMEM((ib * nk, tn), F32), pltpu.VMEM((ib * nk, tn), F32),
                        pltpu.VMEM((ib * nk, tn), BF16), pltpu.VMEM((ib * nk, tn), BF16),
                        pltpu.VMEM((d, tn), F32)],
        compiler_params=_cparams(2),
        name="peer_dense",
    )(x1, na, bm, e1, e2, u_bf, vt_bf, ln_g, ln_b)


def _pick(n, pref):
    return pref if n % pref == 0 else n


def _trunk_layer(x, conv_buf, gla_state, prm, alpha):
    b, l, d = x.shape
    t = b * l
    xf = x.reshape(t, d)
    tm = _pick(t, 256)

    u, qk, v, g, la = _inproj(xf, prm["w_main"], prm["w_f"], prm["gw_pad"], prm["gate_b"], tm=tm)
    conv_dim = u.shape[-1]

    buf_pad = jnp.pad(conv_buf, ((0, 0), (HALO - (CONV_WIDTH - 1), 0), (0, 0)))
    y_conv, conv_new = _conv_mixer(u.reshape(b, l, conv_dim), buf_pad, prm["conv_w"], prm["conv_b"],
                                   prm["conv_ln_g"], prm["conv_ln_b"], tl=_pick(l, 256))

    c = _pick(l, GLA_CHUNK)
    s0_t = jnp.swapaxes(gla_state, -1, -2)
    y_gla, s_t = _gla_mixer(qk.reshape(b, l, -1), v.reshape(b, l, -1), g.reshape(b, l, -1),
                            la.reshape(b, l, -1), s0_t, prm["gla_norm_g"], c=c)
    gla_new = jnp.swapaxes(s_t, -1, -2)

    x1 = _outproj(xf, y_conv.reshape(t, -1), y_gla.reshape(t, -1), prm["w_out_c"], prm["w_out_g"],
                  prm["ln1_g"], prm["ln1_b"], tm=tm, alpha=alpha)

    na, bm, e1, e2 = _route(x1, prm["wq"], prm["keys"], tm=tm)
    y = _peer_dense(x1, na, bm, e1, e2, prm["u_bf"], prm["vt_bf"], prm["ln2_g"], prm["ln2_b"],
                    tn=_pick(t, 512), ib=8, alpha=alpha)
    return y.reshape(b, l, d), conv_new, gla_new


def _prep_layer(i, w_in, conv_w, conv_b, conv_ln_g, conv_ln_b, gla_gate_w, gla_gate_b, gla_norm_g,
                w_out, ln1_g, ln1_b, peer_w_query, peer_sub_keys, peer_u, peer_v, ln2_g, ln2_b):
    conv_dim = conv_w.shape[-1]
    half = conv_dim // 2
    wi = w_in[i]
    a_w, g_w, rest = wi[:, :conv_dim], wi[:, conv_dim:2 * conv_dim], wi[:, 2 * conv_dim:]
    n_rest = rest.shape[1] - GLA_RANK
    w_main = jnp.concatenate([a_w[:, :half], g_w[:, :half], a_w[:, half:], g_w[:, half:],
                              rest[:, :n_rest]], axis=1).astype(BF16)
    w_f = jnp.pad(rest[:, n_rest:], ((0, 0), (0, LANES - GLA_RANK))).astype(BF16)
    gw_pad = jnp.pad(gla_gate_w[i], ((0, LANES - GLA_RANK), (0, 0)))
    row = lambda a: a[i][None, :]
    keys = peer_sub_keys[i]
    return dict(
        w_main=w_main, w_f=w_f, gw_pad=gw_pad, gate_b=row(gla_gate_b),
        conv_w=conv_w[i], conv_b=row(conv_b), conv_ln_g=row(conv_ln_g), conv_ln_b=row(conv_ln_b),
        gla_norm_g=row(gla_norm_g),
        w_out_c=w_out[i][:conv_dim].astype(BF16), w_out_g=w_out[i][conv_dim:].astype(BF16),
        ln1_g=row(ln1_g), ln1_b=row(ln1_b),
        wq=peer_w_query[i].astype(BF16),
        keys=keys.reshape(keys.shape[0] * 2, keys.shape[2], keys.shape[3]),
        u_bf=peer_u[i].astype(BF16), vt_bf=peer_v[i].T.astype(BF16),
        ln2_g=row(ln2_g), ln2_b=row(ln2_b),
    )


def kernel(x_prompt, x_sample, state_conv, state_gla, w_in, conv_w, conv_b, conv_ln_g, conv_ln_b,
           gla_gate_w, gla_gate_b, gla_norm_g, w_out, ln1_g, ln1_b,
           peer_w_query, peer_sub_keys, peer_u, peer_v, ln2_g, ln2_b):
    depth = w_in.shape[0]
    alpha = (2.0 * depth) ** 0.25
    yp, ys = x_prompt, x_sample
    conv_p, gla_p, conv_s, gla_s = [], [], [], []
    for i in range(depth):
        prm = _prep_layer(i, w_in, conv_w, conv_b, conv_ln_g, conv_ln_b, gla_gate_w, gla_gate_b,
                          gla_norm_g, w_out, ln1_g, ln1_b, peer_w_query, peer_sub_keys,
                          peer_u, peer_v, ln2_g, ln2_b)
        bp = x_prompt.shape[0]
        zero_conv = jnp.zeros((bp,) + state_conv.shape[2:], x_prompt.dtype)
        zero_gla = jnp.zeros((bp,) + state_gla.shape[2:], state_gla.dtype)
        yp, cp, gp = _trunk_layer(yp, zero_conv, zero_gla, prm, alpha)
        ys, cs, gs = _trunk_layer(ys, state_conv[i], state_gla[i], prm, alpha)
        conv_p.append(cp)
        gla_p.append(gp)
        conv_s.append(cs)
        gla_s.append(gs)
    return (yp, ys, jnp.stack(conv_p), jnp.stack(gla_p), jnp.stack(conv_s), jnp.stack(gla_s))
```

```python
import functools

import jax
import jax.numpy as jnp
from jax import lax
from jax.experimental import pallas as pl
from jax.experimental.pallas import tpu as pltpu

F32 = jnp.float32
BF16 = jnp.bfloat16

CONV_WIDTH = 31
GLA_HEADS = 4
GLA_RANK = 16
GLA_TAU = 16.0
PEER_HEADS = 8
PEER_NKEYS = 128
PEER_TOPK = 16
LN_EPS = 1e-5

GLA_CHUNK = 64
GLA_SUB = 16
LANES = 128
SUBLANES = 8
HALO = 32
NOT_LISTED = 8192.0
VMEM_LIMIT = 56 * 1024 * 1024


def _cparams(n_axes):
    return pltpu.CompilerParams(dimension_semantics=("arbitrary",) * n_axes,
                                vmem_limit_bytes=VMEM_LIMIT)


def _layer_norm(x, g, b):
    mu = jnp.mean(x, axis=-1, keepdims=True)
    xc = x - mu
    var = jnp.mean(xc * xc, axis=-1, keepdims=True)
    return xc * lax.rsqrt(var + LN_EPS) * g + b


def _sigmoid(x):
    return 1.0 / (1.0 + jnp.exp(-x))


def _log_sigmoid(x):
    return jnp.minimum(x, 0.0) - jnp.log(1.0 + jnp.exp(-jnp.abs(x)))


def _dot_nt(a, b):
    return lax.dot_general(a, b, (((1,), (1,)), ((), ())), preferred_element_type=F32)


def _dot_tn(a, b):
    return lax.dot_general(a, b, (((0,), (0,)), ((), ())), preferred_element_type=F32)


def _split3(x):
    hi = x.astype(BF16)
    r = x - hi.astype(F32)
    mid = r.astype(BF16)
    lo = (r - mid.astype(F32)).astype(BF16)
    return hi, mid, lo


def _inproj_kernel(x_ref, w_ref, wf_ref, gw_ref, gb_ref,
                   u_ref, qk_ref, v_ref, g_ref, la_ref, xb_ref, *, half):
    j = pl.program_id(1)

    @pl.when(j == 0)
    def _():
        xb_ref[...] = x_ref[...].astype(BF16)

    z = jnp.dot(xb_ref[...], w_ref[...], preferred_element_type=F32)

    @pl.when(j == 0)
    def _():
        u_ref[:, :half] = z[:, :half] * _sigmoid(z[:, half:])

    @pl.when(j == 1)
    def _():
        u_ref[:, half:] = z[:, :half] * _sigmoid(z[:, half:])

    @pl.when(j == 2)
    def _():
        qk_ref[...] = z.astype(qk_ref.dtype)

    @pl.when(j == 3)
    def _():
        v_ref[...] = z.astype(v_ref.dtype)

    @pl.when(j == 4)
    def _():
        g_ref[...] = z.astype(g_ref.dtype)
        f = jnp.dot(xb_ref[...], wf_ref[...], preferred_element_type=F32)
        pre = jnp.dot(f, gw_ref[...], preferred_element_type=F32,
                      precision=lax.Precision.HIGHEST) + gb_ref[...]
        la_ref[...] = _log_sigmoid(pre) * (1.0 / GLA_TAU)


def _inproj(x, w_main, w_f, gw_pad, gb, *, tm):
    t, d = x.shape
    nblk = w_main.shape[1] // 1024
    conv_dim = 1024
    qk2 = 1024
    v_dim = 1024
    out_shapes = (
        jax.ShapeDtypeStruct((t, conv_dim), F32),
        jax.ShapeDtypeStruct((t, qk2), BF16),
        jax.ShapeDtypeStruct((t, v_dim), BF16),
        jax.ShapeDtypeStruct((t, v_dim), BF16),
        jax.ShapeDtypeStruct((t, qk2 // 2), F32),
    )
    row = lambda i, j: (i, 0)
    fixed = lambda i, j: (0, 0)
    return pl.pallas_call(
        functools.partial(_inproj_kernel, half=conv_dim // 2),
        grid=(t // tm, nblk),
        in_specs=[
            pl.BlockSpec((tm, d), row),
            pl.BlockSpec((d, 1024), lambda i, j: (0, j)),
            pl.BlockSpec((d, LANES), fixed),
            pl.BlockSpec((LANES, qk2 // 2), fixed),
            pl.BlockSpec((1, qk2 // 2), fixed),
        ],
        out_specs=[
            pl.BlockSpec((tm, conv_dim), row),
            pl.BlockSpec((tm, qk2), row),
            pl.BlockSpec((tm, v_dim), row),
            pl.BlockSpec((tm, v_dim), row),
            pl.BlockSpec((tm, qk2 // 2), row),
        ],
        out_shape=out_shapes,
        scratch_shapes=[pltpu.VMEM((tm, d), BF16)],
        compiler_params=_cparams(2),
        name="inproj",
    )(x, w_main, w_f, gw_pad, gb)


def _conv_kernel(buf_ref, u_ref, w_ref, cb_ref, lg_ref, lb_ref, y_ref, st_ref, ext_ref, sh_ref, wb_ref,
                 *, tl, rs):
    l = pl.program_id(1)

    @pl.when(l == 0)
    def _():
        ext_ref[0:HALO, :] = buf_ref[0]

    @pl.when(l > 0)
    def _():
        ext_ref[0:HALO, :] = ext_ref[tl:tl + HALO, :]

    ext_ref[HALO:HALO + tl, :] = u_ref[0]
    first = HALO - (CONV_WIDTH - 1)
    span = sh_ref.shape[1]
    for r in range(1, SUBLANES):
        sh_ref[r - 1] = ext_ref[r:r + span, :]

    def window(row, n):
        shift = row % SUBLANES
        base = row - shift
        return ext_ref[base:base + n, :] if shift == 0 else sh_ref[shift - 1, base:base + n, :]

    @pl.when((pl.program_id(0) == 0) & (l == 0))
    def _():
        for j in range(CONV_WIDTH):
            wb_ref[j] = jnp.broadcast_to(w_ref[j:j + 1, :], wb_ref.shape[1:])

    for r0 in range(0, tl, rs):
        acc = window(first + r0, rs) * wb_ref[0]
        for j in range(1, CONV_WIDTH):
            acc = acc + window(first + r0 + j, rs) * wb_ref[j]
        y = _layer_norm(acc + cb_ref[...], lg_ref[...], lb_ref[...])
        y_ref[0, r0:r0 + rs, :] = (y * _sigmoid(y)).astype(y_ref.dtype)
    st_ref[0] = ext_ref[tl + first:tl + HALO, :]


def _conv_mixer(u, buf_pad, conv_w, conv_b, ln_g, ln_b, *, tl):
    b, l, c = u.shape
    rs = min(16, tl)
    fixed = lambda i, j: (0, 0)
    return pl.pallas_call(
        functools.partial(_conv_kernel, tl=tl, rs=rs),
        grid=(b, l // tl),
        in_specs=[
            pl.BlockSpec((1, HALO, c), lambda i, j: (i, 0, 0)),
            pl.BlockSpec((1, tl, c), lambda i, j: (i, j, 0)),
            pl.BlockSpec((CONV_WIDTH, c), fixed),
            pl.BlockSpec((1, c), fixed),
            pl.BlockSpec((1, c), fixed),
            pl.BlockSpec((1, c), fixed),
        ],
        out_specs=[
            pl.BlockSpec((1, tl, c), lambda i, j: (i, j, 0)),
            pl.BlockSpec((1, CONV_WIDTH - 1, c), lambda i, j: (i, 0, 0)),
        ],
        out_shape=(jax.ShapeDtypeStruct((b, l, c), BF16),
                   jax.ShapeDtypeStruct((b, CONV_WIDTH - 1, c), F32)),
        scratch_shapes=[pltpu.VMEM((HALO + tl, c), F32),
                        pltpu.VMEM((SUBLANES - 1, tl + HALO - SUBLANES, c), F32),
                        pltpu.VMEM((CONV_WIDTH, rs, c), F32)],
        compiler_params=_cparams(2),
        name="conv_mixer",
    )(buf_pad, u, conv_w, conv_b, ln_g, ln_b)


def _gla_kernel(qk_ref, v_ref, g_ref, la_ref, s0_ref, ng_ref, y_ref, sout_ref, st_ref, *, c, dk, dv):
    l = pl.program_id(1)

    @pl.when(l == 0)
    def _():
        st_ref[...] = s0_ref[0]

    hk = GLA_HEADS * dk
    la = la_ref[0]
    rows = lax.broadcasted_iota(jnp.int32, (c, c), 0)
    cols = lax.broadcasted_iota(jnp.int32, (c, c), 1)
    causal = rows >= cols
    tril = jnp.where(causal, 1.0, 0.0).astype(BF16)
    hi, mid, lo = _split3(la)
    bcum = (jnp.dot(tril, hi, preferred_element_type=F32)
            + jnp.dot(tril, mid, preferred_element_type=F32)
            + jnp.dot(tril, lo, preferred_element_type=F32))
    b_last = bcum[c - 1:c, :]
    q = qk_ref[0, :, :hk].astype(F32) * (dk ** -0.5)
    k = qk_ref[0, :, hk:].astype(F32)
    q_in = (q * jnp.exp(bcum)).astype(BF16)
    k_out = (k * jnp.exp(b_last - bcum)).astype(BF16)
    row_id = lax.broadcasted_iota(jnp.int32, (c, dk), 0)

    for h in range(GLA_HEADS):
        ks = slice(h * dk, (h + 1) * dk)
        vs = slice(h * dv, (h + 1) * dv)
        bh = bcum[:, ks]
        qh = q[:, ks]
        kh = k[:, ks]
        vh = v_ref[0, :, vs].astype(BF16)
        st = st_ref[h]
        o = _dot_nt(q_in[:, ks], st.astype(BF16))
        blocks = []
        for i in range(c // GLA_SUB):
            r0 = i * GLA_SUB
            base = bh[r0 - 1:r0, :] if i > 0 else jnp.zeros((1, dk), F32)
            q_t = qh[r0:r0 + GLA_SUB, :] * jnp.exp(bh[r0:r0 + GLA_SUB, :] - base)
            k_t = kh * jnp.exp(jnp.where(row_id < r0 + GLA_SUB, base - bh, -1e30))
            blocks.append(_dot_nt(q_t.astype(BF16), k_t.astype(BF16)))
        scores = jnp.where(causal, jnp.concatenate(blocks, axis=0), 0.0)
        o = o + jnp.dot(scores.astype(BF16), vh, preferred_element_type=F32)
        st_ref[h] = st * jnp.exp(b_last[:, ks]) + _dot_tn(vh, k_out[:, ks])
        o = o * lax.rsqrt(jnp.mean(o * o, axis=-1, keepdims=True) + LN_EPS)
        gh = g_ref[0, :, vs].astype(F32)
        y_ref[0, :, vs] = (o * ng_ref[:, vs] * (gh * _sigmoid(gh))).astype(y_ref.dtype)

    sout_ref[0] = st_ref[...]


def _gla_mixer(qk, v, g, la, s0_t, norm_g, *, c):
    b, l, hk2 = qk.shape
    dk = hk2 // (2 * GLA_HEADS)
    dv = v.shape[-1] // GLA_HEADS
    tile = lambda i, j: (i, j, 0)
    return pl.pallas_call(
        functools.partial(_gla_kernel, c=c, dk=dk, dv=dv),
        grid=(b, l // c),
        in_specs=[
            pl.BlockSpec((1, c, hk2), tile),
            pl.BlockSpec((1, c, GLA_HEADS * dv), tile),
            pl.BlockSpec((1, c, GLA_HEADS * dv), tile),
            pl.BlockSpec((1, c, hk2 // 2), tile),
            pl.BlockSpec((1, GLA_HEADS, dv, dk), lambda i, j: (i, 0, 0, 0)),
            pl.BlockSpec((1, GLA_HEADS * dv), lambda i, j: (0, 0)),
        ],
        out_specs=[
            pl.BlockSpec((1, c, GLA_HEADS * dv), tile),
            pl.BlockSpec((1, GLA_HEADS, dv, dk), lambda i, j: (i, 0, 0, 0)),
        ],
        out_shape=(jax.ShapeDtypeStruct((b, l, GLA_HEADS * dv), BF16),
                   jax.ShapeDtypeStruct((b, GLA_HEADS, dv, dk), F32)),
        scratch_shapes=[pltpu.VMEM((GLA_HEADS, dv, dk), F32)],
        compiler_params=_cparams(2),
        name="gla_mixer",
    )(qk, v, g, la, s0_t, norm_g)


def _outproj_kernel(x_ref, yc_ref, yg_ref, wc_ref, wg_ref, g_ref, b_ref, o_ref, *, alpha):
    mix = (jnp.dot(yc_ref[...].astype(BF16), wc_ref[...], preferred_element_type=F32)
           + jnp.dot(yg_ref[...].astype(BF16), wg_ref[...], preferred_element_type=F32))
    o_ref[...] = _layer_norm(alpha * x_ref[...] + mix, g_ref[...], b_ref[...])


def _outproj(x, y_conv, y_gla, w_c, w_g, ln_g, ln_b, *, tm, alpha):
    t, d = x.shape
    row = lambda i: (i, 0)
    fixed = lambda i: (0, 0)
    return pl.pallas_call(
        functools.partial(_outproj_kernel, alpha=alpha),
        grid=(t // tm,),
        in_specs=[
            pl.BlockSpec((tm, d), row),
            pl.BlockSpec((tm, y_conv.shape[1]), row),
            pl.BlockSpec((tm, y_gla.shape[1]), row),
            pl.BlockSpec(w_c.shape, fixed),
            pl.BlockSpec(w_g.shape, fixed),
            pl.BlockSpec((1, d), fixed),
            pl.BlockSpec((1, d), fixed),
        ],
        out_specs=pl.BlockSpec((tm, d), row),
        out_shape=jax.ShapeDtypeStruct((t, d), F32),
        compiler_params=_cparams(1),
        name="outproj_ln",
    )(x, y_conv, y_gla, w_c, w_g, ln_g, ln_b)


def _extract_topk(s, exact):
    n = s.shape[0]
    rows = lax.broadcasted_iota(jnp.int32, s.shape, 0).astype(F32) if exact else None
    rank = jnp.full(s.shape, NOT_LISTED, F32)
    vals = []
    for r in range(PEER_TOPK):
        m = jnp.max(s, axis=0, keepdims=True)
        if exact:
            first = jnp.min(jnp.where(s == m, rows, float(n)), axis=0, keepdims=True)
            hit = rows == first
        else:
            hit = s == m
        rank = jnp.where(hit, float(r), rank)
        s = jnp.where(hit, -jnp.inf, s)
        vals.append(m)
    return vals, rank


def _dup16(x):
    bits = lax.bitcast_convert_type(x.astype(BF16).astype(F32), jnp.uint32)
    return lax.bitcast_convert_type(bits | (bits >> 16), jnp.int32)


def _route_head(s1, s2, exact):
    k = PEER_TOPK
    assert k == 16
    vals1, rank1 = _extract_topk(s1, exact)
    vals2, rank2 = _extract_topk(s2, exact)
    v2 = jnp.concatenate(vals2, axis=0)
    v1_tail = jnp.concatenate(vals1[8:], axis=0)
    ev2 = jnp.exp(v2 - vals2[0])
    ev1 = [jnp.exp(vals1[a] - vals1[0]) for a in range(8)]
    ev1_tail = jnp.exp(v1_tail - vals1[0])
    cand = jnp.concatenate([vals1[0] + v2] + [vals1[a] + v2[0:8] for a in range(1, 8)]
                           + [v1_tail + vals2[0]], axis=0)
    gate = jnp.concatenate([ev1[0] * ev2] + [ev1[a] * ev2[0:8] for a in range(1, 8)]
                           + [ev1_tail * ev2[0:1]], axis=0)
    r = lax.broadcasted_iota(jnp.int32, cand.shape, 0)
    a_mid = 1 + ((r - 16) >> 3)
    b_mid = (r - 16) & 7
    head_rows = r < 16
    tail_rows = r >= 72
    flat = jnp.where(head_rows, r, jnp.where(tail_rows, (r - 64) * k, a_mid * k + b_mid)).astype(F32)
    valid = head_rows | tail_rows | ((a_mid + 1) * (b_mid + 1) <= k)
    cand = jnp.where(valid, cand, -jnp.inf)
    sel = jnp.zeros(cand.shape, F32)
    for _ in range(k):
        m = jnp.max(cand, axis=0, keepdims=True)
        if exact:
            first = jnp.min(jnp.where(cand == m, flat, float(k * k)), axis=0, keepdims=True)
            hit = flat == first
        else:
            hit = cand == m
        sel = jnp.where(hit, 1.0, sel)
        cand = jnp.where(hit, -jnp.inf, cand)
    z = jnp.sum(sel * gate, axis=0, keepdims=True)
    heights = [jnp.sum(sel[0:16], axis=0, keepdims=True)]
    heights += [jnp.sum(sel[8 + 8 * a:16 + 8 * a], axis=0, keepdims=True) for a in range(1, 8)]
    heights += [sel[64 + a:65 + a] for a in range(8, 16)]
    na = jnp.full(rank1.shape, NOT_LISTED, F32)
    for a in range(k):
        na = jnp.where(rank1 == float(a), 1.0 - heights[a], na)
    bm = jnp.where(rank2 < float(k), -rank2, -2.0 * NOT_LISTED)
    e1 = jnp.exp(s1 - vals1[0]) / z
    e2 = jnp.exp(s2 - vals2[0])
    count = lambda rank: jnp.sum(jnp.where(rank < NOT_LISTED, 1.0, 0.0), axis=0, keepdims=True)
    overfull = ((count(rank1) != float(k)) | (count(rank2) != float(k))
                | (jnp.sum(sel, axis=0, keepdims=True) != float(k)))
    return na, bm, e1, e2, overfull


def _route_kernel(x_ref, wq_ref, keys_ref, na_ref, bm_ref, e1_ref, e2_ref, s_ref):
    xq = jnp.dot(x_ref[...].astype(BF16), wq_ref[...], preferred_element_type=F32)
    dsub = keys_ref.shape[-1]
    for h in range(PEER_HEADS):
        for p in range(2):
            off = (h * 2 + p) * dsub
            qh, qm, ql = _split3(xq[:, off:off + dsub])
            kh, km, kl = _split3(keys_ref[h * 2 + p])
            s_ref[p] = (_dot_nt(kh, qh) + (_dot_nt(kh, qm) + _dot_nt(km, qh))
                        + (_dot_nt(kh, ql) + _dot_nt(km, qm) + _dot_nt(kl, qh)))

        def emit(exact, h=h):
            na, bm, e1, e2, overfull = _route_head(s_ref[0], s_ref[1], exact)
            na_ref[h] = _dup16(na)
            bm_ref[h] = bm.astype(BF16)
            e1_ref[h] = _dup16(e1)
            e2_ref[h] = e2.astype(BF16)
            return overfull

        overfull = emit(False)
        n_bad = jnp.sum(jnp.where(overfull, 1.0, 0.0))

        @pl.when(n_bad > 0.0)
        def _():
            emit(True)


def _route(x1, wq, keys, *, tm):
    t, d = x1.shape
    nk = keys.shape[1]
    ospec = pl.BlockSpec((PEER_HEADS, nk, tm), lambda i: (0, 0, i))
    words = jax.ShapeDtypeStruct((PEER_HEADS, nk, t), jnp.int32)
    halfs = jax.ShapeDtypeStruct((PEER_HEADS, nk, t), BF16)
    return pl.pallas_call(
        _route_kernel,
        grid=(t // tm,),
        in_specs=[
            pl.BlockSpec((tm, d), lambda i: (i, 0)),
            pl.BlockSpec(wq.shape, lambda i: (0, 0)),
            pl.BlockSpec(keys.shape, lambda i: (0, 0, 0)),
        ],
        out_specs=[ospec] * 4,
        out_shape=(words, halfs, words, halfs),
        scratch_shapes=[pltpu.VMEM((2, nk, tm), F32)],
        compiler_params=_cparams(1),
        name="peer_route",
    )(x1, wq, keys)


def _gelu(x):
    return 0.5 * x * (1.0 + lax.erf(x * 0.7071067811865476))


def _peer_kernel(x_ref, na_ref, bm_ref, e1_ref, e2_ref, u_ref, vt_ref, g_ref, b_ref, y_ref,
                 xt_ref, h0_ref, h1_ref, p0_ref, p1_ref, acc_ref, *, ib, nblk, alpha):
    step = pl.program_id(1)
    nk = bm_ref.shape[1]
    tn = xt_ref.shape[1]

    def stage(h_new, h_old, p_new, p_old, project, weigh, combine):
        parts = 4
        group = 16
        hm = ib * nk // parts
        dm = acc_ref.shape[0] // parts
        for q in range(parts):
            if project:
                h_new[q * hm:(q + 1) * hm, :] = jnp.dot(u_ref[q * hm:(q + 1) * hm, :], xt_ref[...],
                                                        preferred_element_type=F32)
            if weigh:
                for ii in range(q * ib // parts, (q + 1) * ib // parts):
                    row = lambda ref, h: pltpu.bitcast(
                        jnp.broadcast_to(ref[h, ii:ii + 1, :], (group // 2, tn)), BF16)
                    na_b = [row(na_ref, h) for h in range(PEER_HEADS)]
                    e1_b = [row(e1_ref, h) for h in range(PEER_HEADS)]
                    for r0 in range(0, nk, group):
                        cols = slice(r0, r0 + group)
                        w = jnp.zeros((group, tn), BF16)
                        for h in range(PEER_HEADS):
                            w = w + jnp.where(bm_ref[h, cols, :] >= na_b[h], e2_ref[h, cols, :] * e1_b[h],
                                              jnp.zeros((group, tn), BF16))
                        rows = slice(ii * nk + r0, ii * nk + r0 + group)
                        p_new[rows, :] = _gelu(h_old[rows, :]).astype(BF16) * w
            if combine:
                acc_ref[q * dm:(q + 1) * dm, :] += jnp.dot(vt_ref[q * dm:(q + 1) * dm, :], p_old[...],
                                                           preferred_element_type=F32)

    @pl.when(step == 0)
    def _():
        xt_ref[...] = x_ref[...].T.astype(BF16)
        acc_ref[...] = jnp.zeros_like(acc_ref)
        stage(h0_ref, None, None, None, True, False, False)

    @pl.when(step == 1)
    def _():
        stage(h1_ref, h0_ref, p1_ref, None, True, True, False)

    mid = (step >= 2) & (step < nblk)

    @pl.when(mid & (step % 2 == 0))
    def _():
        stage(h0_ref, h1_ref, p0_ref, p1_ref, True, True, True)

    @pl.when(mid & (step % 2 == 1))
    def _():
        stage(h1_ref, h0_ref, p1_ref, p0_ref, True, True, True)

    @pl.when(step == nblk)
    def _():
        stage(None, h1_ref, p0_ref, p1_ref, False, True, True)

    @pl.when(step == nblk + 1)
    def _():
        stage(None, None, None, p0_ref, False, False, True)
        y_ref[...] = _layer_norm(alpha * x_ref[...] + acc_ref[...].T, g_ref[...], b_ref[...])


def _peer_dense(x1, na, bm, e1, e2, u_bf, vt_bf, ln_g, ln_b, *, tn, ib, alpha):
    t, d = x1.shape
    nk = bm.shape[1]
    nblk = u_bf.shape[0] // (ib * nk)
    assert nblk % 2 == 0 and nblk >= 4
    last = nblk - 1
    weigh_blk = lambda s: jnp.clip(s - 1, 0, last)
    row_code = pl.BlockSpec((PEER_HEADS, ib, tn), lambda i, s: (0, weigh_blk(s), i))
    col_code = pl.BlockSpec((PEER_HEADS, nk, tn), lambda i, s: (0, 0, i))
    return pl.pallas_call(
        functools.partial(_peer_kernel, ib=ib, nblk=nblk, alpha=alpha),
        grid=(t // tn, nblk + 2),
        in_specs=[
            pl.BlockSpec((tn, d), lambda i, s: (i, 0)),
            row_code, col_code, row_code, col_code,
            pl.BlockSpec((ib * nk, d), lambda i, s: (jnp.minimum(s, last), 0)),
            pl.BlockSpec((d, ib * nk), lambda i, s: (0, jnp.clip(s - 2, 0, last))),
            pl.BlockSpec((1, d), lambda i, s: (0, 0)),
            pl.BlockSpec((1, d), lambda i, s: (0, 0)),
        ],
        out_specs=pl.BlockSpec((tn, d), lambda i, s: (i, 0)),
        out_shape=jax.ShapeDtypeStruct((t, d), F32),
        scratch_shapes=[pltpu.VMEM((d, tn), BF16),
                        pltpu.VMEM((ib * nk, tn), F32), pltpu.VMEM((ib * nk, tn), F32),
                        pltpu.VMEM((ib * nk, tn), BF16), pltpu.VMEM((ib * nk, tn), BF16),
                        pltpu.VMEM((d, tn), F32)],
        compiler_params=_cparams(2),
        name="peer_dense",
    )(x1, na, bm, e1, e2, u_bf, vt_bf, ln_g, ln_b)


def _pick(n, pref):
    return pref if n % pref == 0 else n


def _trunk_layer(x, conv_buf, gla_state, prm, alpha):
    b, l, d = x.shape
    t = b * l
    xf = x.reshape(t, d)
    tm = _pick(t, 256)

    u, qk, v, g, la = _inproj(xf, prm["w_main"], prm["w_f"], prm["gw_pad"], prm["gate_b"],
                              tm=_pick(t, 512))
    conv_dim = u.shape[-1]

    buf_pad = jnp.pad(conv_buf, ((0, 0), (HALO - (CONV_WIDTH - 1), 0), (0, 0)))
    y_conv, conv_new = _conv_mixer(u.reshape(b, l, conv_dim), buf_pad, prm["conv_w"], prm["conv_b"],
                                   prm["conv_ln_g"], prm["conv_ln_b"], tl=_pick(l, 256))

    c = _pick(l, GLA_CHUNK)
    s0_t = jnp.swapaxes(gla_state, -1, -2)
    y_gla, s_t = _gla_mixer(qk.reshape(b, l, -1), v.reshape(b, l, -1), g.reshape(b, l, -1),
                            la.reshape(b, l, -1), s0_t, prm["gla_norm_g"], c=c)
    gla_new = jnp.swapaxes(s_t, -1, -2)

    x1 = _outproj(xf, y_conv.reshape(t, -1), y_gla.reshape(t, -1), prm["w_out_c"], prm["w_out_g"],
                  prm["ln1_g"], prm["ln1_b"], tm=tm, alpha=alpha)

    na, bm, e1, e2 = _route(x1, prm["wq"], prm["keys"], tm=tm)
    y = _peer_dense(x1, na, bm, e1, e2, prm["u_bf"], prm["vt_bf"], prm["ln2_g"], prm["ln2_b"],
                    tn=_pick(t, 512), ib=8, alpha=alpha)
    return y.reshape(b, l, d), conv_new, gla_new


def _prep_layer(i, w_in, conv_w, conv_b, conv_ln_g, conv_ln_b, gla_gate_w, gla_gate_b, gla_norm_g,
                w_out, ln1_g, ln1_b, peer_w_query, peer_sub_keys, peer_u, peer_v, ln2_g, ln2_b):
    conv_dim = conv_w.shape[-1]
    half = conv_dim // 2
    wi = w_in[i]
    a_w, g_w, rest = wi[:, :conv_dim], wi[:, conv_dim:2 * conv_dim], wi[:, 2 * conv_dim:]
    n_rest = rest.shape[1] - GLA_RANK
    w_main = jnp.concatenate([a_w[:, :half], g_w[:, :half], a_w[:, half:], g_w[:, half:],
                              rest[:, :n_rest]], axis=1).astype(BF16)
    w_f = jnp.pad(rest[:, n_rest:], ((0, 0), (0, LANES - GLA_RANK))).astype(BF16)
    gw_pad = jnp.pad(gla_gate_w[i], ((0, LANES - GLA_RANK), (0, 0)))
    row = lambda a: a[i][None, :]
    keys = peer_sub_keys[i]
    return dict(
        w_main=w_main, w_f=w_f, gw_pad=gw_pad, gate_b=row(gla_gate_b),
        conv_w=conv_w[i], conv_b=row(conv_b), conv_ln_g=row(conv_ln_g), conv_ln_b=row(conv_ln_b),
        gla_norm_g=row(gla_norm_g),
        w_out_c=w_out[i][:conv_dim].astype(BF16), w_out_g=w_out[i][conv_dim:].astype(BF16),
        ln1_g=row(ln1_g), ln1_b=row(ln1_b),
        wq=peer_w_query[i].astype(BF16),
        keys=keys.reshape(keys.shape[0] * 2, keys.shape[2], keys.shape[3]),
        u_bf=peer_u[i].astype(BF16), vt_bf=peer_v[i].T.astype(BF16),
        ln2_g=row(ln2_g), ln2_b=row(ln2_b),
    )


def kernel(x_prompt, x_sample, state_conv, state_gla, w_in, conv_w, conv_b, conv_ln_g, conv_ln_b,
           gla_gate_w, gla_gate_b, gla_norm_g, w_out, ln1_g, ln1_b,
           peer_w_query, peer_sub_keys, peer_u, peer_v, ln2_g, ln2_b):
    depth = w_in.shape[0]
    alpha = (2.0 * depth) ** 0.25
    yp, ys = x_prompt, x_sample
    conv_p, gla_p, conv_s, gla_s = [], [], [], []
    for i in range(depth):
        prm = _prep_layer(i, w_in, conv_w, conv_b, conv_ln_g, conv_ln_b, gla_gate_w, gla_gate_b,
                          gla_norm_g, w_out, ln1_g, ln1_b, peer_w_query, peer_sub_keys,
                          peer_u, peer_v, ln2_g, ln2_b)
        bp = x_prompt.shape[0]
        zero_conv = jnp.zeros((bp,) + state_conv.shape[2:], x_prompt.dtype)
        zero_gla = jnp.zeros((bp,) + state_gla.shape[2:], state_gla.dtype)
        yp, cp, gp = _trunk_layer(yp, zero_conv, zero_gla, prm, alpha)
        ys, cs, gs = _trunk_layer(ys, state_conv[i], state_gla[i], prm, alpha)
        conv_p.append(cp)
        gla_p.append(gp)
        conv_s.append(cs)
        gla_s.append(gs)
    return (yp, ys, jnp.stack(conv_p), jnp.stack(gla_p), jnp.stack(conv_s), jnp.stack(gla_s))
```

```python
import functools

import jax
import jax.numpy as jnp
from jax import lax
from jax.experimental import pallas as pl
from jax.experimental.pallas import tpu as pltpu

F32 = jnp.float32
BF16 = jnp.bfloat16

CONV_WIDTH = 31
GLA_HEADS = 4
GLA_RANK = 16
GLA_TAU = 16.0
PEER_HEADS = 8
PEER_NKEYS = 128
PEER_TOPK = 16
LN_EPS = 1e-5

GLA_CHUNK = 64
GLA_SUB = 16
LANES = 128
SUBLANES = 8
HALO = 32
NOT_LISTED = 8192.0
VMEM_LIMIT = 56 * 1024 * 1024


def _cparams(n_axes):
    return pltpu.CompilerParams(dimension_semantics=("arbitrary",) * n_axes,
                                vmem_limit_bytes=VMEM_LIMIT)


def _layer_norm(x, g, b):
    mu = jnp.mean(x, axis=-1, keepdims=True)
    xc = x - mu
    var = jnp.mean(xc * xc, axis=-1, keepdims=True)
    return xc * lax.rsqrt(var + LN_EPS) * g + b


def _sigmoid(x):
    return 1.0 / (1.0 + jnp.exp(-x))


def _log_sigmoid(x):
    return jnp.minimum(x, 0.0) - jnp.log(1.0 + jnp.exp(-jnp.abs(x)))


def _dot_nt(a, b):
    return lax.dot_general(a, b, (((1,), (1,)), ((), ())), preferred_element_type=F32)


def _dot_tn(a, b):
    return lax.dot_general(a, b, (((0,), (0,)), ((), ())), preferred_element_type=F32)


def _split3(x):
    hi = x.astype(BF16)
    r = x - hi.astype(F32)
    mid = r.astype(BF16)
    lo = (r - mid.astype(F32)).astype(BF16)
    return hi, mid, lo


def _inproj_kernel(x_ref, w_ref, wf_ref, gw_ref, gb_ref,
                   u_ref, qk_ref, v_ref, g_ref, la_ref, xb_ref, *, half):
    j = pl.program_id(1)

    @pl.when(j == 0)
    def _():
        xb_ref[...] = x_ref[...].astype(BF16)

    z = jnp.dot(xb_ref[...], w_ref[...], preferred_element_type=F32)

    @pl.when(j == 0)
    def _():
        u_ref[:, :half] = z[:, :half] * _sigmoid(z[:, half:])

    @pl.when(j == 1)
    def _():
        u_ref[:, half:] = z[:, :half] * _sigmoid(z[:, half:])

    @pl.when(j == 2)
    def _():
        qk_ref[...] = z.astype(qk_ref.dtype)

    @pl.when(j == 3)
    def _():
        v_ref[...] = z.astype(v_ref.dtype)

    @pl.when(j == 4)
    def _():
        g_ref[...] = z.astype(g_ref.dtype)
        f = jnp.dot(xb_ref[...], wf_ref[...], preferred_element_type=F32)
        pre = jnp.dot(f, gw_ref[...], preferred_element_type=F32,
                      precision=lax.Precision.HIGHEST) + gb_ref[...]
        la_ref[...] = _log_sigmoid(pre) * (1.0 / GLA_TAU)


def _inproj(x, w_main, w_f, gw_pad, gb, *, tm):
    t, d = x.shape
    nblk = w_main.shape[1] // 1024
    conv_dim = 1024
    qk2 = 1024
    v_dim = 1024
    out_shapes = (
        jax.ShapeDtypeStruct((t, conv_dim), F32),
        jax.ShapeDtypeStruct((t, qk2), BF16),
        jax.ShapeDtypeStruct((t, v_dim), BF16),
        jax.ShapeDtypeStruct((t, v_dim), BF16),
        jax.ShapeDtypeStruct((t, qk2 // 2), F32),
    )
    row = lambda i, j: (i, 0)
    fixed = lambda i, j: (0, 0)
    return pl.pallas_call(
        functools.partial(_inproj_kernel, half=conv_dim // 2),
        grid=(t // tm, nblk),
        in_specs=[
            pl.BlockSpec((tm, d), row),
            pl.BlockSpec((d, 1024), lambda i, j: (0, j)),
            pl.BlockSpec((d, LANES), fixed),
            pl.BlockSpec((LANES, qk2 // 2), fixed),
            pl.BlockSpec((1, qk2 // 2), fixed),
        ],
        out_specs=[
            pl.BlockSpec((tm, conv_dim), row),
            pl.BlockSpec((tm, qk2), row),
            pl.BlockSpec((tm, v_dim), row),
            pl.BlockSpec((tm, v_dim), row),
            pl.BlockSpec((tm, qk2 // 2), row),
        ],
        out_shape=out_shapes,
        scratch_shapes=[pltpu.VMEM((tm, d), BF16)],
        compiler_params=_cparams(2),
        name="inproj",
    )(x, w_main, w_f, gw_pad, gb)


def _conv_kernel(buf_ref, u_ref, w_ref, cb_ref, lg_ref, lb_ref, y_ref, st_ref, ext_ref, sh_ref, wb_ref,
                 *, tl, rs):
    l = pl.program_id(1)

    @pl.when(l == 0)
    def _():
        ext_ref[0:HALO, :] = buf_ref[0]

    @pl.when(l > 0)
    def _():
        ext_ref[0:HALO, :] = ext_ref[tl:tl + HALO, :]

    ext_ref[HALO:HALO + tl, :] = u_ref[0]
    first = HALO - (CONV_WIDTH - 1)
    span = sh_ref.shape[1]
    for r in range(1, SUBLANES):
        sh_ref[r - 1] = ext_ref[r:r + span, :]

    def window(row, n):
        shift = row % SUBLANES
        base = row - shift
        return ext_ref[base:base + n, :] if shift == 0 else sh_ref[shift - 1, base:base + n, :]

    @pl.when((pl.program_id(0) == 0) & (l == 0))
    def _():
        for j in range(CONV_WIDTH):
            wb_ref[j] = jnp.broadcast_to(w_ref[j:j + 1, :], wb_ref.shape[1:])

    for r0 in range(0, tl, rs):
        acc = window(first + r0, rs) * wb_ref[0]
        for j in range(1, CONV_WIDTH):
            acc = acc + window(first + r0 + j, rs) * wb_ref[j]
        y = _layer_norm(acc + cb_ref[...], lg_ref[...], lb_ref[...])
        y_ref[0, r0:r0 + rs, :] = (y * _sigmoid(y)).astype(y_ref.dtype)
    st_ref[0] = ext_ref[tl + first:tl + HALO, :]


def _conv_mixer(u, buf_pad, conv_w, conv_b, ln_g, ln_b, *, tl):
    b, l, c = u.shape
    rs = min(16, tl)
    fixed = lambda i, j: (0, 0)
    return pl.pallas_call(
        functools.partial(_conv_kernel, tl=tl, rs=rs),
        grid=(b, l // tl),
        in_specs=[
            pl.BlockSpec((1, HALO, c), lambda i, j: (i, 0, 0)),
            pl.BlockSpec((1, tl, c), lambda i, j: (i, j, 0)),
            pl.BlockSpec((CONV_WIDTH, c), fixed),
            pl.BlockSpec((1, c), fixed),
            pl.BlockSpec((1, c), fixed),
            pl.BlockSpec((1, c), fixed),
        ],
        out_specs=[
            pl.BlockSpec((1, tl, c), lambda i, j: (i, j, 0)),
            pl.BlockSpec((1, CONV_WIDTH - 1, c), lambda i, j: (i, 0, 0)),
        ],
        out_shape=(jax.ShapeDtypeStruct((b, l, c), BF16),
                   jax.ShapeDtypeStruct((b, CONV_WIDTH - 1, c), F32)),
        scratch_shapes=[pltpu.VMEM((HALO + tl, c), F32),
                        pltpu.VMEM((SUBLANES - 1, tl + HALO - SUBLANES, c), F32),
                        pltpu.VMEM((CONV_WIDTH, rs, c), F32)],
        compiler_params=_cparams(2),
        name="conv_mixer",
    )(buf_pad, u, conv_w, conv_b, ln_g, ln_b)


def _gla_kernel(qk_ref, v_ref, g_ref, la_ref, s0_ref, ng_ref, y_ref, sout_ref, st_ref, *, c, dk, dv):
    l = pl.program_id(1)

    @pl.when(l == 0)
    def _():
        st_ref[...] = s0_ref[0]

    hk = GLA_HEADS * dk
    la = la_ref[0]
    rows = lax.broadcasted_iota(jnp.int32, (c, c), 0)
    cols = lax.broadcasted_iota(jnp.int32, (c, c), 1)
    causal = rows >= cols
    tril = jnp.where(causal, 1.0, 0.0).astype(BF16)
    hi, mid, lo = _split3(la)
    bcum = (jnp.dot(tril, hi, preferred_element_type=F32)
            + jnp.dot(tril, mid, preferred_element_type=F32)
            + jnp.dot(tril, lo, preferred_element_type=F32))
    b_last = bcum[c - 1:c, :]
    q = qk_ref[0, :, :hk].astype(F32) * (dk ** -0.5)
    k = qk_ref[0, :, hk:].astype(F32)
    q_in = (q * jnp.exp(bcum)).astype(BF16)
    k_out = (k * jnp.exp(b_last - bcum)).astype(BF16)
    row_id = lax.broadcasted_iota(jnp.int32, (c, dk), 0)

    for h in range(GLA_HEADS):
        ks = slice(h * dk, (h + 1) * dk)
        vs = slice(h * dv, (h + 1) * dv)
        bh = bcum[:, ks]
        qh = q[:, ks]
        kh = k[:, ks]
        vh = v_ref[0, :, vs].astype(BF16)
        st = st_ref[h]
        o = _dot_nt(q_in[:, ks], st.astype(BF16))
        blocks = []
        for i in range(c // GLA_SUB):
            r0 = i * GLA_SUB
            base = bh[r0 - 1:r0, :] if i > 0 else jnp.zeros((1, dk), F32)
            q_t = qh[r0:r0 + GLA_SUB, :] * jnp.exp(bh[r0:r0 + GLA_SUB, :] - base)
            k_t = kh * jnp.exp(jnp.where(row_id < r0 + GLA_SUB, base - bh, -1e30))
            blocks.append(_dot_nt(q_t.astype(BF16), k_t.astype(BF16)))
        scores = jnp.where(causal, jnp.concatenate(blocks, axis=0), 0.0)
        o = o + jnp.dot(scores.astype(BF16), vh, preferred_element_type=F32)
        st_ref[h] = st * jnp.exp(b_last[:, ks]) + _dot_tn(vh, k_out[:, ks])
        o = o * lax.rsqrt(jnp.mean(o * o, axis=-1, keepdims=True) + LN_EPS)
        gh = g_ref[0, :, vs].astype(F32)
        y_ref[0, :, vs] = (o * ng_ref[:, vs] * (gh * _sigmoid(gh))).astype(y_ref.dtype)

    sout_ref[0] = st_ref[...]


def _gla_mixer(qk, v, g, la, s0_t, norm_g, *, c):
    b, l, hk2 = qk.shape
    dk = hk2 // (2 * GLA_HEADS)
    dv = v.shape[-1] // GLA_HEADS
    tile = lambda i, j: (i, j, 0)
    return pl.pallas_call(
        functools.partial(_gla_kernel, c=c, dk=dk, dv=dv),
        grid=(b, l // c),
        in_specs=[
            pl.BlockSpec((1, c, hk2), tile),
            pl.BlockSpec((1, c, GLA_HEADS * dv), tile),
            pl.BlockSpec((1, c, GLA_HEADS * dv), tile),
            pl.BlockSpec((1, c, hk2 // 2), tile),
            pl.BlockSpec((1, GLA_HEADS, dv, dk), lambda i, j: (i, 0, 0, 0)),
            pl.BlockSpec((1, GLA_HEADS * dv), lambda i, j: (0, 0)),
        ],
        out_specs=[
            pl.BlockSpec((1, c, GLA_HEADS * dv), tile),
            pl.BlockSpec((1, GLA_HEADS, dv, dk), lambda i, j: (i, 0, 0, 0)),
        ],
        out_shape=(jax.ShapeDtypeStruct((b, l, GLA_HEADS * dv), BF16),
                   jax.ShapeDtypeStruct((b, GLA_HEADS, dv, dk), F32)),
        scratch_shapes=[pltpu.VMEM((GLA_HEADS, dv, dk), F32)],
        compiler_params=_cparams(2),
        name="gla_mixer",
    )(qk, v, g, la, s0_t, norm_g)


def _outproj_kernel(x_ref, yc_ref, yg_ref, wc_ref, wg_ref, g_ref, b_ref, o_ref, *, alpha):
    mix = (jnp.dot(yc_ref[...].astype(BF16), wc_ref[...], preferred_element_type=F32)
           + jnp.dot(yg_ref[...].astype(BF16), wg_ref[...], preferred_element_type=F32))
    o_ref[...] = _layer_norm(alpha * x_ref[...] + mix, g_ref[...], b_ref[...])


def _outproj(x, y_conv, y_gla, w_c, w_g, ln_g, ln_b, *, tm, alpha):
    t, d = x.shape
    row = lambda i: (i, 0)
    fixed = lambda i: (0, 0)
    return pl.pallas_call(
        functools.partial(_outproj_kernel, alpha=alpha),
        grid=(t // tm,),
        in_specs=[
            pl.BlockSpec((tm, d), row),
            pl.BlockSpec((tm, y_conv.shape[1]), row),
            pl.BlockSpec((tm, y_gla.shape[1]), row),
            pl.BlockSpec(w_c.shape, fixed),
            pl.BlockSpec(w_g.shape, fixed),
            pl.BlockSpec((1, d), fixed),
            pl.BlockSpec((1, d), fixed),
        ],
        out_specs=pl.BlockSpec((tm, d), row),
        out_shape=jax.ShapeDtypeStruct((t, d), F32),
        compiler_params=_cparams(1),
        name="outproj_ln",
    )(x, y_conv, y_gla, w_c, w_g, ln_g, ln_b)


def _extract_topk(s):
    n = s.shape[0]
    rows = lax.broadcasted_iota(jnp.int32, s.shape, 0).astype(F32)
    rank = jnp.full(s.shape, NOT_LISTED, F32)
    vals = []
    for r in range(PEER_TOPK):
        m = jnp.max(s, axis=0, keepdims=True)
        first = jnp.min(jnp.where(s == m, rows, float(n)), axis=0, keepdims=True)
        hit = rows == first
        rank = jnp.where(hit, float(r), rank)
        s = jnp.where(hit, -jnp.inf, s)
        vals.append(m)
    return vals, rank


def _sort_pairs(lo, hi):
    def merge(lo, hi, r):
        step = r * 2
        if step < hi - lo:
            yield from merge(lo, hi, step)
            yield from merge(lo + r, hi, step)
            yield from [(i, i + r) for i in range(lo + r, hi - r, step)]
        else:
            yield (lo, lo + r)

    if hi - lo >= 1:
        mid = lo + (hi - lo) // 2
        yield from _sort_pairs(lo, mid)
        yield from _sort_pairs(mid + 1, hi)
        yield from merge(lo, hi, 1)


def _topk_values(s):
    k = PEER_TOPK
    v = [s[r:r + SUBLANES, :] for r in range(0, s.shape[0], SUBLANES)]
    assert len(v) >= k and len(v) & (len(v) - 1) == 0
    for i, j in _sort_pairs(0, len(v) - 1):
        v[i], v[j] = jnp.maximum(v[i], v[j]), jnp.minimum(v[i], v[j])
    vals = []
    taken = jnp.zeros(v[0].shape, F32)
    for t in range(k):
        m = jnp.max(v[0], axis=0, keepdims=True)
        hit = v[0] == m
        taken = taken + jnp.where(hit, 1.0, 0.0)
        for r in range(k - 1 - t):
            v[r] = jnp.where(hit, v[r + 1], v[r])
        vals.append(m)
    equal = jnp.sum(taken, axis=0, keepdims=True) != float(k)
    for t in range(k - 1):
        equal = equal | (vals[t] == vals[t + 1])
    at_least = jnp.sum(jnp.where(s >= vals[k - 1], 1.0, 0.0), axis=0, keepdims=True)
    return vals, equal | (at_least != float(k))


def _dup16(x):
    bits = lax.bitcast_convert_type(x.astype(BF16).astype(F32), jnp.uint32)
    return lax.bitcast_convert_type(bits | (bits >> 16), jnp.int32)


def _route_head(s1, s2, exact):
    k = PEER_TOPK
    assert k == 16
    if exact:
        vals1, rank1 = _extract_topk(s1)
        vals2, rank2 = _extract_topk(s2)
        unusable = None
    else:
        vals1, equal1 = _topk_values(s1)
        vals2, equal2 = _topk_values(s2)
        unusable = equal1 | equal2
    v2 = jnp.concatenate(vals2, axis=0)
    v1_tail = jnp.concatenate(vals1[8:], axis=0)
    ev2 = jnp.exp(v2 - vals2[0])
    ev1 = [jnp.exp(vals1[a] - vals1[0]) for a in range(8)]
    ev1_tail = jnp.exp(v1_tail - vals1[0])
    cand = jnp.concatenate([vals1[0] + v2] + [vals1[a] + v2[0:8] for a in range(1, 8)]
                           + [v1_tail + vals2[0]], axis=0)
    gate = jnp.concatenate([ev1[0] * ev2] + [ev1[a] * ev2[0:8] for a in range(1, 8)]
                           + [ev1_tail * ev2[0:1]], axis=0)
    r = lax.broadcasted_iota(jnp.int32, cand.shape, 0)
    a_mid = 1 + ((r - 16) >> 3)
    b_mid = (r - 16) & 7
    head_rows = r < 16
    tail_rows = r >= 72
    flat = jnp.where(head_rows, r, jnp.where(tail_rows, (r - 64) * k, a_mid * k + b_mid)).astype(F32)
    valid = head_rows | tail_rows | ((a_mid + 1) * (b_mid + 1) <= k)
    cand = jnp.where(valid, cand, -jnp.inf)
    sel = jnp.zeros(cand.shape, F32)
    for _ in range(k):
        m = jnp.max(cand, axis=0, keepdims=True)
        if exact:
            first = jnp.min(jnp.where(cand == m, flat, float(k * k)), axis=0, keepdims=True)
            hit = flat == first
        else:
            hit = cand == m
        sel = jnp.where(hit, 1.0, sel)
        cand = jnp.where(hit, -jnp.inf, cand)
    z = jnp.sum(sel * gate, axis=0, keepdims=True)
    heights = [jnp.sum(sel[0:16], axis=0, keepdims=True)]
    heights += [jnp.sum(sel[8 + 8 * a:16 + 8 * a], axis=0, keepdims=True) for a in range(1, 8)]
    heights += [sel[64 + a:65 + a] for a in range(8, 16)]
    na = jnp.full(s1.shape, NOT_LISTED, F32)
    bm = jnp.full(s2.shape, -2.0 * NOT_LISTED, F32)
    for a in range(k):
        na = jnp.where((rank1 == float(a)) if exact else (s1 == vals1[a]), 1.0 - heights[a], na)
        bm = jnp.where((rank2 == float(a)) if exact else (s2 == vals2[a]), -float(a), bm)
    e1 = jnp.exp(s1 - vals1[0]) / z
    e2 = jnp.exp(s2 - vals2[0])
    if not exact:
        unusable = unusable | (jnp.sum(sel, axis=0, keepdims=True) != float(k))
    return na, bm, e1, e2, unusable


def _route_kernel(x_ref, wq_ref, keys_ref, na_ref, bm_ref, e1_ref, e2_ref, s_ref):
    xq = jnp.dot(x_ref[...].astype(BF16), wq_ref[...], preferred_element_type=F32)
    dsub = keys_ref.shape[-1]
    for h in range(PEER_HEADS):
        for p in range(2):
            off = (h * 2 + p) * dsub
            qh, qm, ql = _split3(xq[:, off:off + dsub])
            kh, km, kl = _split3(keys_ref[h * 2 + p])
            s_ref[p] = (_dot_nt(kh, qh) + (_dot_nt(kh, qm) + _dot_nt(km, qh))
                        + (_dot_nt(kh, ql) + _dot_nt(km, qm) + _dot_nt(kl, qh)))

        def emit(exact, h=h):
            na, bm, e1, e2, unusable = _route_head(s_ref[0], s_ref[1], exact)
            na_ref[h] = _dup16(na)
            bm_ref[h] = bm.astype(BF16)
            e1_ref[h] = _dup16(e1)
            e2_ref[h] = e2.astype(BF16)
            return unusable

        unusable = emit(False)

        @pl.when(jnp.sum(jnp.where(unusable, 1.0, 0.0)) > 0.0)
        def _():
            emit(True)


def _route(x1, wq, keys, *, tm):
    t, d = x1.shape
    nk = keys.shape[1]
    ospec = pl.BlockSpec((PEER_HEADS, nk, tm), lambda i: (0, 0, i))
    words = jax.ShapeDtypeStruct((PEER_HEADS, nk, t), jnp.int32)
    halfs = jax.ShapeDtypeStruct((PEER_HEADS, nk, t), BF16)
    return pl.pallas_call(
        _route_kernel,
        grid=(t // tm,),
        in_specs=[
            pl.BlockSpec((tm, d), lambda i: (i, 0)),
            pl.BlockSpec(wq.shape, lambda i: (0, 0)),
            pl.BlockSpec(keys.shape, lambda i: (0, 0, 0)),
        ],
        out_specs=[ospec] * 4,
        out_shape=(words, halfs, words, halfs),
        scratch_shapes=[pltpu.VMEM((2, nk, tm), F32)],
        compiler_params=_cparams(1),
        name="peer_route",
    )(x1, wq, keys)


def _gelu(x):
    return 0.5 * x * (1.0 + lax.erf(x * 0.7071067811865476))


def _peer_kernel(x_ref, na_ref, bm_ref, e1_ref, e2_ref, u_ref, vt_ref, g_ref, b_ref, y_ref,
                 xt_ref, h0_ref, h1_ref, p0_ref, p1_ref, acc_ref, *, ib, nblk, alpha):
    step = pl.program_id(1)
    nk = bm_ref.shape[1]
    tn = xt_ref.shape[1]

    def stage(h_new, h_old, p_new, p_old, project, weigh, combine):
        parts = 4
        group = 16
        hm = ib * nk // parts
        dm = acc_ref.shape[0] // parts
        for q in range(parts):
            if project:
                h_new[q * hm:(q + 1) * hm, :] = jnp.dot(u_ref[q * hm:(q + 1) * hm, :], xt_ref[...],
                                                        preferred_element_type=F32)
            if weigh:
                for ii in range(q * ib // parts, (q + 1) * ib // parts):
                    row = lambda ref, h: pltpu.bitcast(
                        jnp.broadcast_to(ref[h, ii:ii + 1, :], (group // 2, tn)), BF16)
                    na_b = [row(na_ref, h) for h in range(PEER_HEADS)]
                    e1_b = [row(e1_ref, h) for h in range(PEER_HEADS)]
                    for r0 in range(0, nk, group):
                        cols = slice(r0, r0 + group)
                        w = jnp.zeros((group, tn), BF16)
                        for h in range(PEER_HEADS):
                            w = w + jnp.where(bm_ref[h, cols, :] >= na_b[h], e2_ref[h, cols, :] * e1_b[h],
                                              jnp.zeros((group, tn), BF16))
                        rows = slice(ii * nk + r0, ii * nk + r0 + group)
                        p_new[rows, :] = _gelu(h_old[rows, :]).astype(BF16) * w
            if combine:
                acc_ref[q * dm:(q + 1) * dm, :] += jnp.dot(vt_ref[q * dm:(q + 1) * dm, :], p_old[...],
                                                           preferred_element_type=F32)

    @pl.when(step == 0)
    def _():
        for c0 in range(0, tn, LANES):
            xt_ref[:, c0:c0 + LANES] = x_ref[c0:c0 + LANES, :].T.astype(BF16)
        acc_ref[...] = jnp.zeros_like(acc_ref)
        stage(h0_ref, None, None, None, True, False, False)

    @pl.when(step == 1)
    def _():
        stage(h1_ref, h0_ref, p1_ref, None, True, True, False)

    mid = (step >= 2) & (step < nblk)

    @pl.when(mid & (step % 2 == 0))
    def _():
        stage(h0_ref, h1_ref, p0_ref, p1_ref, True, True, True)

    @pl.when(mid & (step % 2 == 1))
    def _():
        stage(h1_ref, h0_ref, p1_ref, p0_ref, True, True, True)

    @pl.when(step == nblk)
    def _():
        stage(None, h1_ref, p0_ref, p1_ref, False, True, True)

    @pl.when(step == nblk + 1)
    def _():
        stage(None, None, None, p0_ref, False, False, True)
        for c0 in range(0, tn, LANES):
            rows = slice(c0, c0 + LANES)
            y_ref[rows, :] = _layer_norm(alpha * x_ref[rows, :] + acc_ref[:, rows].T, g_ref[...], b_ref[...])


def _peer_dense(x1, na, bm, e1, e2, u_bf, vt_bf, ln_g, ln_b, *, tn, ib, alpha):
    t, d = x1.shape
    nk = bm.shape[1]
    nblk = u_bf.shape[0] // (ib * nk)
    assert nblk % 2 == 0 and nblk >= 4
    last = nblk - 1
    weigh_blk = lambda s: jnp.clip(s - 1, 0, last)
    row_code = pl.BlockSpec((PEER_HEADS, ib, tn), lambda i, s: (0, weigh_blk(s), i))
    col_code = pl.BlockSpec((PEER_HEADS, nk, tn), lambda i, s: (0, 0, i))
    return pl.pallas_call(
        functools.partial(_peer_kernel, ib=ib, nblk=nblk, alpha=alpha),
        grid=(t // tn, nblk + 2),
        in_specs=[
            pl.BlockSpec((tn, d), lambda i, s: (i, 0)),
            row_code, col_code, row_code, col_code,
            pl.BlockSpec((ib * nk, d), lambda i, s: (jnp.minimum(s, last), 0)),
            pl.BlockSpec((d, ib * nk), lambda i, s: (0, jnp.clip(s - 2, 0, last))),
            pl.BlockSpec((1, d), lambda i, s: (0, 0)),
            pl.BlockSpec((1, d), lambda i, s: (0, 0)),
        ],
        out_specs=pl.BlockSpec((tn, d), lambda i, s: (i, 0)),
        out_shape=jax.ShapeDtypeStruct((t, d), F32),
        scratch_shapes=[pltpu.VMEM((d, tn), BF16),
                        pltpu.VMEM((ib * nk, tn), F32), pltpu.VMEM((ib * nk, tn), F32),
                        pltpu.VMEM((ib * nk, tn), BF16), pltpu.VMEM((ib * nk, tn), BF16),
                        pltpu.VMEM((d, tn), F32)],
        compiler_params=_cparams(2),
        name="peer_dense",
    )(x1, na, bm, e1, e2, u_bf, vt_bf, ln_g, ln_b)


def _pick(n, pref):
    return pref if n % pref == 0 else n


def _trunk_layer(x, conv_buf, gla_state, prm, alpha):
    b, l, d = x.shape
    t = b * l
    xf = x.reshape(t, d)
    tm = _pick(t, 256)

    u, qk, v, g, la = _inproj(xf, prm["w_main"], prm["w_f"], prm["gw_pad"], prm["gate_b"],
                              tm=_pick(t, 512))
    conv_dim = u.shape[-1]

    buf_pad = jnp.pad(conv_buf, ((0, 0), (HALO - (CONV_WIDTH - 1), 0), (0, 0)))
    y_conv, conv_new = _conv_mixer(u.reshape(b, l, conv_dim), buf_pad, prm["conv_w"], prm["conv_b"],
                                   prm["conv_ln_g"], prm["conv_ln_b"], tl=_pick(l, 256))

    c = _pick(l, GLA_CHUNK)
    s0_t = jnp.swapaxes(gla_state, -1, -2)
    y_gla, s_t = _gla_mixer(qk.reshape(b, l, -1), v.reshape(b, l, -1), g.reshape(b, l, -1),
                            la.reshape(b, l, -1), s0_t, prm["gla_norm_g"], c=c)
    gla_new = jnp.swapaxes(s_t, -1, -2)

    x1 = _outproj(xf, y_conv.reshape(t, -1), y_gla.reshape(t, -1), prm["w_out_c"], prm["w_out_g"],
                  prm["ln1_g"], prm["ln1_b"], tm=tm, alpha=alpha)

    na, bm, e1, e2 = _route(x1, prm["wq"], prm["keys"], tm=tm)
    y = _peer_dense(x1, na, bm, e1, e2, prm["u_bf"], prm["vt_bf"], prm["ln2_g"], prm["ln2_b"],
                    tn=_pick(t, 512), ib=8, alpha=alpha)
    return y.reshape(b, l, d), conv_new, gla_new


def _prep_layer(i, w_in, conv_w, conv_b, conv_ln_g, conv_ln_b, gla_gate_w, gla_gate_b, gla_norm_g,
                w_out, ln1_g, ln1_b, peer_w_query, peer_sub_keys, peer_u, peer_v, ln2_g, ln2_b):
    conv_dim = conv_w.shape[-1]
    half = conv_dim // 2
    wi = w_in[i]
    a_w, g_w, rest = wi[:, :conv_dim], wi[:, conv_dim:2 * conv_dim], wi[:, 2 * conv_dim:]
    n_rest = rest.shape[1] - GLA_RANK
    w_main = jnp.concatenate([a_w[:, :half], g_w[:, :half], a_w[:, half:], g_w[:, half:],
                              rest[:, :n_rest]], axis=1).astype(BF16)
    w_f = jnp.pad(rest[:, n_rest:], ((0, 0), (0, LANES - GLA_RANK))).astype(BF16)
    gw_pad = jnp.pad(gla_gate_w[i], ((0, LANES - GLA_RANK), (0, 0)))
    row = lambda a: a[i][None, :]
    keys = peer_sub_keys[i]
    return dict(
        w_main=w_main, w_f=w_f, gw_pad=gw_pad, gate_b=row(gla_gate_b),
        conv_w=conv_w[i], conv_b=row(conv_b), conv_ln_g=row(conv_ln_g), conv_ln_b=row(conv_ln_b),
        gla_norm_g=row(gla_norm_g),
        w_out_c=w_out[i][:conv_dim].astype(BF16), w_out_g=w_out[i][conv_dim:].astype(BF16),
        ln1_g=row(ln1_g), ln1_b=row(ln1_b),
        wq=peer_w_query[i].astype(BF16),
        keys=keys.reshape(keys.shape[0] * 2, keys.shape[2], keys.shape[3]),
        u_bf=peer_u[i].astype(BF16), vt_bf=peer_v[i].T.astype(BF16),
        ln2_g=row(ln2_g), ln2_b=row(ln2_b),
    )


def kernel(x_prompt, x_sample, state_conv, state_gla, w_in, conv_w, conv_b, conv_ln_g, conv_ln_b,
           gla_gate_w, gla_gate_b, gla_norm_g, w_out, ln1_g, ln1_b,
           peer_w_query, peer_sub_keys, peer_u, peer_v, ln2_g, ln2_b):
    depth = w_in.shape[0]
    alpha = (2.0 * depth) ** 0.25
    yp, ys = x_prompt, x_sample
    conv_p, gla_p, conv_s, gla_s = [], [], [], []
    for i in range(depth):
        prm = _prep_layer(i, w_in, conv_w, conv_b, conv_ln_g, conv_ln_b, gla_gate_w, gla_gate_b,
                          gla_norm_g, w_out, ln1_g, ln1_b, peer_w_query, peer_sub_keys,
                          peer_u, peer_v, ln2_g, ln2_b)
        bp = x_prompt.shape[0]
        zero_conv = jnp.zeros((bp,) + state_conv.shape[2:], x_prompt.dtype)
        zero_gla = jnp.zeros((bp,) + state_gla.shape[2:], state_gla.dtype)
        yp, cp, gp = _trunk_layer(yp, zero_conv, zero_gla, prm, alpha)
        ys, cs, gs = _trunk_layer(ys, state_conv[i], state_gla[i], prm, alpha)
        conv_p.append(cp)
        gla_p.append(gp)
        conv_s.append(cs)
        gla_s.append(gs)
    return (yp, ys, jnp.stack(conv_p), jnp.stack(gla_p), jnp.stack(conv_s), jnp.stack(gla_s))
```

```python
import functools

import jax
import jax.numpy as jnp
from jax import lax
from jax.experimental import pallas as pl
from jax.experimental.pallas import tpu as pltpu

F32 = jnp.float32
BF16 = jnp.bfloat16

CONV_WIDTH = 31
GLA_HEADS = 4
GLA_RANK = 16
GLA_TAU = 16.0
PEER_HEADS = 8
PEER_NKEYS = 128
PEER_TOPK = 16
LN_EPS = 1e-5

GLA_CHUNK = 64
GLA_SUB = 16
LANES = 128
SUBLANES = 8
HALO = 32
NOT_LISTED = 8192.0
VMEM_LIMIT = 56 * 1024 * 1024


def _cparams(n_axes):
    return pltpu.CompilerParams(dimension_semantics=("arbitrary",) * n_axes,
                                vmem_limit_bytes=VMEM_LIMIT)


def _layer_norm(x, g, b):
    mu = jnp.mean(x, axis=-1, keepdims=True)
    xc = x - mu
    var = jnp.mean(xc * xc, axis=-1, keepdims=True)
    return xc * lax.rsqrt(var + LN_EPS) * g + b


def _sigmoid(x):
    return 1.0 / (1.0 + jnp.exp(-x))


def _log_sigmoid(x):
    return jnp.minimum(x, 0.0) - jnp.log(1.0 + jnp.exp(-jnp.abs(x)))


def _dot_nt(a, b):
    return lax.dot_general(a, b, (((1,), (1,)), ((), ())), preferred_element_type=F32)


def _dot_tn(a, b):
    return lax.dot_general(a, b, (((0,), (0,)), ((), ())), preferred_element_type=F32)


def _pack_rows(w):
    r, c = w.shape
    pairs = jnp.swapaxes(w.astype(BF16).reshape(r // 2, 2, c), -1, -2)
    return lax.bitcast_convert_type(pairs, jnp.uint32)


def _unpack_rows(words_ref, row0, nrows):
    return pltpu.bitcast(words_ref[row0 // 2:(row0 + nrows) // 2, :], BF16)


def _split3(x):
    hi = x.astype(BF16)
    r = x - hi.astype(F32)
    mid = r.astype(BF16)
    lo = (r - mid.astype(F32)).astype(BF16)
    return hi, mid, lo


def _inproj_kernel(x_ref, w_ref, wf_ref, gw_ref, gb_ref,
                   u_ref, qk_ref, v_ref, g_ref, la_ref, xb_ref, *, half):
    j = pl.program_id(1)

    @pl.when(j == 0)
    def _():
        xb_ref[...] = x_ref[...].astype(BF16)

    z = jnp.dot(xb_ref[...], w_ref[...], preferred_element_type=F32)

    @pl.when(j == 0)
    def _():
        u_ref[:, :half] = z[:, :half] * _sigmoid(z[:, half:])

    @pl.when(j == 1)
    def _():
        u_ref[:, half:] = z[:, :half] * _sigmoid(z[:, half:])

    @pl.when(j == 2)
    def _():
        qk_ref[...] = z.astype(qk_ref.dtype)

    @pl.when(j == 3)
    def _():
        v_ref[...] = z.astype(v_ref.dtype)

    @pl.when(j == 4)
    def _():
        g_ref[...] = z.astype(g_ref.dtype)
        f = jnp.dot(xb_ref[...], wf_ref[...], preferred_element_type=F32)
        pre = jnp.dot(f, gw_ref[...], preferred_element_type=F32,
                      precision=lax.Precision.HIGHEST) + gb_ref[...]
        la_ref[...] = _log_sigmoid(pre) * (1.0 / GLA_TAU)


def _inproj(x, w_main, w_f, gw_pad, gb, *, tm):
    t, d = x.shape
    nblk = w_main.shape[1] // 1024
    conv_dim = 1024
    qk2 = 1024
    v_dim = 1024
    out_shapes = (
        jax.ShapeDtypeStruct((t, conv_dim), F32),
        jax.ShapeDtypeStruct((t, qk2), BF16),
        jax.ShapeDtypeStruct((t, v_dim), BF16),
        jax.ShapeDtypeStruct((t, v_dim), BF16),
        jax.ShapeDtypeStruct((t, qk2 // 2), F32),
    )
    row = lambda i, j: (i, 0)
    fixed = lambda i, j: (0, 0)
    return pl.pallas_call(
        functools.partial(_inproj_kernel, half=conv_dim // 2),
        grid=(t // tm, nblk),
        in_specs=[
            pl.BlockSpec((tm, d), row),
            pl.BlockSpec((d, 1024), lambda i, j: (0, j)),
            pl.BlockSpec((d, LANES), fixed),
            pl.BlockSpec((LANES, qk2 // 2), fixed),
            pl.BlockSpec((1, qk2 // 2), fixed),
        ],
        out_specs=[
            pl.BlockSpec((tm, conv_dim), row),
            pl.BlockSpec((tm, qk2), row),
            pl.BlockSpec((tm, v_dim), row),
            pl.BlockSpec((tm, v_dim), row),
            pl.BlockSpec((tm, qk2 // 2), row),
        ],
        out_shape=out_shapes,
        scratch_shapes=[pltpu.VMEM((tm, d), BF16)],
        compiler_params=_cparams(2),
        name="inproj",
    )(x, w_main, w_f, gw_pad, gb)


def _conv_kernel(buf_ref, u_ref, w_ref, cb_ref, lg_ref, lb_ref, y_ref, st_ref, ext_ref, sh_ref, wb_ref,
                 *, tl, rs):
    l = pl.program_id(1)

    @pl.when(l == 0)
    def _():
        ext_ref[0:HALO, :] = buf_ref[0]

    @pl.when(l > 0)
    def _():
        ext_ref[0:HALO, :] = ext_ref[tl:tl + HALO, :]

    ext_ref[HALO:HALO + tl, :] = u_ref[0]
    first = HALO - (CONV_WIDTH - 1)
    span = sh_ref.shape[1]
    for r in range(1, SUBLANES):
        sh_ref[r - 1] = ext_ref[r:r + span, :]

    def window(row, n):
        shift = row % SUBLANES
        base = row - shift
        return ext_ref[base:base + n, :] if shift == 0 else sh_ref[shift - 1, base:base + n, :]

    @pl.when((pl.program_id(0) == 0) & (l == 0))
    def _():
        for j in range(CONV_WIDTH):
            wb_ref[j] = jnp.broadcast_to(w_ref[j:j + 1, :], wb_ref.shape[1:])

    for r0 in range(0, tl, rs):
        acc = window(first + r0, rs) * wb_ref[0]
        for j in range(1, CONV_WIDTH):
            acc = acc + window(first + r0 + j, rs) * wb_ref[j]
        y = _layer_norm(acc + cb_ref[...], lg_ref[...], lb_ref[...])
        y_ref[0, r0:r0 + rs, :] = (y * _sigmoid(y)).astype(y_ref.dtype)
    st_ref[0] = ext_ref[tl + first:tl + HALO, :]


def _conv_mixer(u, buf_pad, conv_w, conv_b, ln_g, ln_b, *, tl):
    b, l, c = u.shape
    rs = min(16, tl)
    fixed = lambda i, j: (0, 0)
    return pl.pallas_call(
        functools.partial(_conv_kernel, tl=tl, rs=rs),
        grid=(b, l // tl),
        in_specs=[
            pl.BlockSpec((1, HALO, c), lambda i, j: (i, 0, 0)),
            pl.BlockSpec((1, tl, c), lambda i, j: (i, j, 0)),
            pl.BlockSpec((CONV_WIDTH, c), fixed),
            pl.BlockSpec((1, c), fixed),
            pl.BlockSpec((1, c), fixed),
            pl.BlockSpec((1, c), fixed),
        ],
        out_specs=[
            pl.BlockSpec((1, tl, c), lambda i, j: (i, j, 0)),
            pl.BlockSpec((1, CONV_WIDTH - 1, c), lambda i, j: (i, 0, 0)),
        ],
        out_shape=(jax.ShapeDtypeStruct((b, l, c), BF16),
                   jax.ShapeDtypeStruct((b, CONV_WIDTH - 1, c), F32)),
        scratch_shapes=[pltpu.VMEM((HALO + tl, c), F32),
                        pltpu.VMEM((SUBLANES - 1, tl + HALO - SUBLANES, c), F32),
                        pltpu.VMEM((CONV_WIDTH, rs, c), F32)],
        compiler_params=_cparams(2),
        name="conv_mixer",
    )(buf_pad, u, conv_w, conv_b, ln_g, ln_b)


def _gla_kernel(qk_ref, v_ref, g_ref, la_ref, s0_ref, ng_ref, y_ref, sout_ref, st_ref, *, c, dk, dv):
    l = pl.program_id(1)

    @pl.when(l == 0)
    def _():
        st_ref[...] = s0_ref[0]

    hk = GLA_HEADS * dk
    la = la_ref[0]
    rows = lax.broadcasted_iota(jnp.int32, (c, c), 0)
    cols = lax.broadcasted_iota(jnp.int32, (c, c), 1)
    causal = rows >= cols
    tril = jnp.where(causal, 1.0, 0.0).astype(BF16)
    hi, mid, lo = _split3(la)
    bcum = (jnp.dot(tril, hi, preferred_element_type=F32)
            + jnp.dot(tril, mid, preferred_element_type=F32)
            + jnp.dot(tril, lo, preferred_element_type=F32))
    b_last = bcum[c - 1:c, :]
    q = qk_ref[0, :, :hk].astype(F32) * (dk ** -0.5)
    k = qk_ref[0, :, hk:].astype(F32)
    q_in = (q * jnp.exp(bcum)).astype(BF16)
    k_out = (k * jnp.exp(b_last - bcum)).astype(BF16)
    row_id = lax.broadcasted_iota(jnp.int32, (c, dk), 0)

    for h in range(GLA_HEADS):
        ks = slice(h * dk, (h + 1) * dk)
        vs = slice(h * dv, (h + 1) * dv)
        bh = bcum[:, ks]
        qh = q[:, ks]
        kh = k[:, ks]
        vh = v_ref[0, :, vs].astype(BF16)
        st = st_ref[h]
        o = _dot_nt(q_in[:, ks], st.astype(BF16))
        blocks = []
        for i in range(c // GLA_SUB):
            r0 = i * GLA_SUB
            base = bh[r0 - 1:r0, :] if i > 0 else jnp.zeros((1, dk), F32)
            q_t = qh[r0:r0 + GLA_SUB, :] * jnp.exp(bh[r0:r0 + GLA_SUB, :] - base)
            k_t = kh * jnp.exp(jnp.where(row_id < r0 + GLA_SUB, base - bh, -1e30))
            blocks.append(_dot_nt(q_t.astype(BF16), k_t.astype(BF16)))
        scores = jnp.where(causal, jnp.concatenate(blocks, axis=0), 0.0)
        o = o + jnp.dot(scores.astype(BF16), vh, preferred_element_type=F32)
        st_ref[h] = st * jnp.exp(b_last[:, ks]) + _dot_tn(vh, k_out[:, ks])
        o = o * lax.rsqrt(jnp.mean(o * o, axis=-1, keepdims=True) + LN_EPS)
        gh = g_ref[0, :, vs].astype(F32)
        y_ref[0, :, vs] = (o * ng_ref[:, vs] * (gh * _sigmoid(gh))).astype(y_ref.dtype)

    sout_ref[0] = st_ref[...]


def _gla_mixer(qk, v, g, la, s0_t, norm_g, *, c):
    b, l, hk2 = qk.shape
    dk = hk2 // (2 * GLA_HEADS)
    dv = v.shape[-1] // GLA_HEADS
    tile = lambda i, j: (i, j, 0)
    return pl.pallas_call(
        functools.partial(_gla_kernel, c=c, dk=dk, dv=dv),
        grid=(b, l // c),
        in_specs=[
            pl.BlockSpec((1, c, hk2), tile),
            pl.BlockSpec((1, c, GLA_HEADS * dv), tile),
            pl.BlockSpec((1, c, GLA_HEADS * dv), tile),
            pl.BlockSpec((1, c, hk2 // 2), tile),
            pl.BlockSpec((1, GLA_HEADS, dv, dk), lambda i, j: (i, 0, 0, 0)),
            pl.BlockSpec((1, GLA_HEADS * dv), lambda i, j: (0, 0)),
        ],
        out_specs=[
            pl.BlockSpec((1, c, GLA_HEADS * dv), tile),
            pl.BlockSpec((1, GLA_HEADS, dv, dk), lambda i, j: (i, 0, 0, 0)),
        ],
        out_shape=(jax.ShapeDtypeStruct((b, l, GLA_HEADS * dv), BF16),
                   jax.ShapeDtypeStruct((b, GLA_HEADS, dv, dk), F32)),
        scratch_shapes=[pltpu.VMEM((GLA_HEADS, dv, dk), F32)],
        compiler_params=_cparams(2),
        name="gla_mixer",
    )(qk, v, g, la, s0_t, norm_g)


def _outproj_kernel(x_ref, yc_ref, yg_ref, wc_ref, wg_ref, g_ref, b_ref, o_ref, *, alpha):
    mix = (jnp.dot(yc_ref[...].astype(BF16), wc_ref[...], preferred_element_type=F32)
           + jnp.dot(yg_ref[...].astype(BF16), wg_ref[...], preferred_element_type=F32))
    o_ref[...] = _layer_norm(alpha * x_ref[...] + mix, g_ref[...], b_ref[...])


def _outproj(x, y_conv, y_gla, w_c, w_g, ln_g, ln_b, *, tm, alpha):
    t, d = x.shape
    row = lambda i: (i, 0)
    fixed = lambda i: (0, 0)
    return pl.pallas_call(
        functools.partial(_outproj_kernel, alpha=alpha),
        grid=(t // tm,),
        in_specs=[
            pl.BlockSpec((tm, d), row),
            pl.BlockSpec((tm, y_conv.shape[1]), row),
            pl.BlockSpec((tm, y_gla.shape[1]), row),
            pl.BlockSpec(w_c.shape, fixed),
            pl.BlockSpec(w_g.shape, fixed),
            pl.BlockSpec((1, d), fixed),
            pl.BlockSpec((1, d), fixed),
        ],
        out_specs=pl.BlockSpec((tm, d), row),
        out_shape=jax.ShapeDtypeStruct((t, d), F32),
        compiler_params=_cparams(1),
        name="outproj_ln",
    )(x, y_conv, y_gla, w_c, w_g, ln_g, ln_b)


def _extract_topk(s):
    n = s.shape[0]
    rows = lax.broadcasted_iota(jnp.int32, s.shape, 0).astype(F32)
    rank = jnp.full(s.shape, NOT_LISTED, F32)
    vals = []
    for r in range(PEER_TOPK):
        m = jnp.max(s, axis=0, keepdims=True)
        first = jnp.min(jnp.where(s == m, rows, float(n)), axis=0, keepdims=True)
        hit = rows == first
        rank = jnp.where(hit, float(r), rank)
        s = jnp.where(hit, -jnp.inf, s)
        vals.append(m)
    return vals, rank


def _sort_pairs(lo, hi):
    def merge(lo, hi, r):
        step = r * 2
        if step < hi - lo:
            yield from merge(lo, hi, step)
            yield from merge(lo + r, hi, step)
            yield from [(i, i + r) for i in range(lo + r, hi - r, step)]
        else:
            yield (lo, lo + r)

    if hi - lo >= 1:
        mid = lo + (hi - lo) // 2
        yield from _sort_pairs(lo, mid)
        yield from _sort_pairs(mid + 1, hi)
        yield from merge(lo, hi, 1)


def _topk_values(s):
    k = PEER_TOPK
    v = [s[r:r + SUBLANES, :] for r in range(0, s.shape[0], SUBLANES)]
    assert len(v) >= k and len(v) & (len(v) - 1) == 0
    for i, j in _sort_pairs(0, len(v) - 1):
        v[i], v[j] = jnp.maximum(v[i], v[j]), jnp.minimum(v[i], v[j])
    vals = []
    taken = jnp.zeros(v[0].shape, F32)
    for t in range(k):
        m = jnp.max(v[0], axis=0, keepdims=True)
        hit = v[0] == m
        taken = taken + jnp.where(hit, 1.0, 0.0)
        for r in range(k - 1 - t):
            v[r] = jnp.where(hit, v[r + 1], v[r])
        vals.append(m)
    equal = jnp.sum(taken, axis=0, keepdims=True) != float(k)
    for t in range(k - 1):
        equal = equal | (vals[t] == vals[t + 1])
    at_least = jnp.sum(jnp.where(s >= vals[k - 1], 1.0, 0.0), axis=0, keepdims=True)
    return vals, equal | (at_least != float(k))


def _dup16(x):
    bits = lax.bitcast_convert_type(x.astype(BF16).astype(F32), jnp.uint32)
    return lax.bitcast_convert_type(bits | (bits >> 16), jnp.int32)


def _route_head(s1, s2, exact):
    k = PEER_TOPK
    assert k == 16
    if exact:
        vals1, rank1 = _extract_topk(s1)
        vals2, rank2 = _extract_topk(s2)
        unusable = None
    else:
        vals1, equal1 = _topk_values(s1)
        vals2, equal2 = _topk_values(s2)
        unusable = equal1 | equal2
    v2 = jnp.concatenate(vals2, axis=0)
    v1_tail = jnp.concatenate(vals1[8:], axis=0)
    ev2 = jnp.exp(v2 - vals2[0])
    ev1 = [jnp.exp(vals1[a] - vals1[0]) for a in range(8)]
    ev1_tail = jnp.exp(v1_tail - vals1[0])
    cand = jnp.concatenate([vals1[0] + v2] + [vals1[a] + v2[0:8] for a in range(1, 8)]
                           + [v1_tail + vals2[0]], axis=0)
    gate = jnp.concatenate([ev1[0] * ev2] + [ev1[a] * ev2[0:8] for a in range(1, 8)]
                           + [ev1_tail * ev2[0:1]], axis=0)
    r = lax.broadcasted_iota(jnp.int32, cand.shape, 0)
    a_mid = 1 + ((r - 16) >> 3)
    b_mid = (r - 16) & 7
    head_rows = r < 16
    tail_rows = r >= 72
    flat = jnp.where(head_rows, r, jnp.where(tail_rows, (r - 64) * k, a_mid * k + b_mid)).astype(F32)
    valid = head_rows | tail_rows | ((a_mid + 1) * (b_mid + 1) <= k)
    cand = jnp.where(valid, cand, -jnp.inf)
    sel = jnp.zeros(cand.shape, F32)
    for _ in range(k):
        m = jnp.max(cand, axis=0, keepdims=True)
        if exact:
            first = jnp.min(jnp.where(cand == m, flat, float(k * k)), axis=0, keepdims=True)
            hit = flat == first
        else:
            hit = cand == m
        sel = jnp.where(hit, 1.0, sel)
        cand = jnp.where(hit, -jnp.inf, cand)
    z = jnp.sum(sel * gate, axis=0, keepdims=True)
    heights = [jnp.sum(sel[0:16], axis=0, keepdims=True)]
    heights += [jnp.sum(sel[8 + 8 * a:16 + 8 * a], axis=0, keepdims=True) for a in range(1, 8)]
    heights += [sel[64 + a:65 + a] for a in range(8, 16)]
    na = jnp.full(s1.shape, NOT_LISTED, F32)
    bm = jnp.full(s2.shape, -2.0 * NOT_LISTED, F32)
    for a in range(k):
        na = jnp.where((rank1 == float(a)) if exact else (s1 == vals1[a]), 1.0 - heights[a], na)
        bm = jnp.where((rank2 == float(a)) if exact else (s2 == vals2[a]), -float(a), bm)
    e1 = jnp.exp(s1 - vals1[0]) / z
    e2 = jnp.exp(s2 - vals2[0])
    if not exact:
        unusable = unusable | (jnp.sum(sel, axis=0, keepdims=True) != float(k))
    return na, bm, e1, e2, unusable


def _route_kernel(x_ref, wq_ref, keys_ref, na_ref, bm_ref, e1_ref, e2_ref, s_ref):
    xq = jnp.dot(x_ref[...].astype(BF16), wq_ref[...], preferred_element_type=F32)
    dsub = keys_ref.shape[-1]
    for h in range(PEER_HEADS):
        for p in range(2):
            off = (h * 2 + p) * dsub
            qh, qm, ql = _split3(xq[:, off:off + dsub])
            kh, km, kl = _split3(keys_ref[h * 2 + p])
            s_ref[p] = (_dot_nt(kh, qh) + (_dot_nt(kh, qm) + _dot_nt(km, qh))
                        + (_dot_nt(kh, ql) + _dot_nt(km, qm) + _dot_nt(kl, qh)))

        def emit(exact, h=h):
            na, bm, e1, e2, unusable = _route_head(s_ref[0], s_ref[1], exact)
            na_ref[h] = _dup16(na)
            bm_ref[h] = bm.astype(BF16)
            e1_ref[h] = _dup16(e1)
            e2_ref[h] = e2.astype(BF16)
            return unusable

        unusable = emit(False)

        @pl.when(jnp.sum(jnp.where(unusable, 1.0, 0.0)) > 0.0)
        def _():
            emit(True)


def _route(x1, wq, keys, *, tm):
    t, d = x1.shape
    nk = keys.shape[1]
    ospec = pl.BlockSpec((PEER_HEADS, nk, tm), lambda i: (0, 0, i))
    words = jax.ShapeDtypeStruct((PEER_HEADS, nk, t), jnp.int32)
    halfs = jax.ShapeDtypeStruct((PEER_HEADS, nk, t), BF16)
    return pl.pallas_call(
        _route_kernel,
        grid=(t // tm,),
        in_specs=[
            pl.BlockSpec((tm, d), lambda i: (i, 0)),
            pl.BlockSpec(wq.shape, lambda i: (0, 0)),
            pl.BlockSpec(keys.shape, lambda i: (0, 0, 0)),
        ],
        out_specs=[ospec] * 4,
        out_shape=(words, halfs, words, halfs),
        scratch_shapes=[pltpu.VMEM((2, nk, tm), F32)],
        compiler_params=_cparams(1),
        name="peer_route",
    )(x1, wq, keys)


def _gelu(x):
    return 0.5 * x * (1.0 + lax.erf(x * 0.7071067811865476))


def _peer_kernel(x_ref, na_ref, bm_ref, e1_ref, e2_ref, u_ref, vt_ref, g_ref, b_ref, y_ref,
                 xt_ref, h0_ref, h1_ref, p0_ref, p1_ref, acc_ref, *, ib, nblk, alpha):
    step = pl.program_id(1)
    nk = bm_ref.shape[1]
    tn = xt_ref.shape[1]

    def stage(h_new, h_old, p_new, p_old, project, weigh, combine):
        parts = 4
        group = 16
        hm = ib * nk // parts
        dm = acc_ref.shape[0] // parts
        for q in range(parts):
            if project:
                h_new[q * hm:(q + 1) * hm, :] = jnp.dot(_unpack_rows(u_ref, q * hm, hm), xt_ref[...],
                                                        preferred_element_type=F32)
            if weigh:
                for ii in range(q * ib // parts, (q + 1) * ib // parts):
                    row = lambda ref, h: pltpu.bitcast(
                        jnp.broadcast_to(ref[h, ii:ii + 1, :], (group // 2, tn)), BF16)
                    na_b = [row(na_ref, h) for h in range(PEER_HEADS)]
                    e1_b = [row(e1_ref, h) for h in range(PEER_HEADS)]
                    for r0 in range(0, nk, group):
                        cols = slice(r0, r0 + group)
                        w = jnp.zeros((group, tn), BF16)
                        for h in range(PEER_HEADS):
                            w = w + jnp.where(bm_ref[h, cols, :] >= na_b[h], e2_ref[h, cols, :] * e1_b[h],
                                              jnp.zeros((group, tn), BF16))
                        rows = slice(ii * nk + r0, ii * nk + r0 + group)
                        p_new[rows, :] = _gelu(h_old[rows, :]).astype(BF16) * w
            if combine:
                acc_ref[q * dm:(q + 1) * dm, :] += jnp.dot(_unpack_rows(vt_ref, q * dm, dm), p_old[...],
                                                           preferred_element_type=F32)

    @pl.when(step == 0)
    def _():
        for c0 in range(0, tn, LANES):
            xt_ref[:, c0:c0 + LANES] = x_ref[c0:c0 + LANES, :].T.astype(BF16)
        acc_ref[...] = jnp.zeros_like(acc_ref)
        stage(h0_ref, None, None, None, True, False, False)

    @pl.when(step == 1)
    def _():
        stage(h1_ref, h0_ref, p1_ref, None, True, True, False)

    mid = (step >= 2) & (step < nblk)

    @pl.when(mid & (step % 2 == 0))
    def _():
        stage(h0_ref, h1_ref, p0_ref, p1_ref, True, True, True)

    @pl.when(mid & (step % 2 == 1))
    def _():
        stage(h1_ref, h0_ref, p1_ref, p0_ref, True, True, True)

    @pl.when(step == nblk)
    def _():
        stage(None, h1_ref, p0_ref, p1_ref, False, True, True)

    @pl.when(step == nblk + 1)
    def _():
        stage(None, None, None, p0_ref, False, False, True)
        for c0 in range(0, tn, LANES):
            rows = slice(c0, c0 + LANES)
            y_ref[rows, :] = _layer_norm(alpha * x_ref[rows, :] + acc_ref[:, rows].T, g_ref[...], b_ref[...])


def _peer_dense(x1, na, bm, e1, e2, u_words, vt_words, ln_g, ln_b, *, tn, ib, alpha):
    t, d = x1.shape
    nk = bm.shape[1]
    nblk = 2 * u_words.shape[0] // (ib * nk)
    assert nblk % 2 == 0 and nblk >= 4
    last = nblk - 1
    weigh_blk = lambda s: jnp.clip(s - 1, 0, last)
    row_code = pl.BlockSpec((PEER_HEADS, ib, tn), lambda i, s: (0, weigh_blk(s), i))
    col_code = pl.BlockSpec((PEER_HEADS, nk, tn), lambda i, s: (0, 0, i))
    return pl.pallas_call(
        functools.partial(_peer_kernel, ib=ib, nblk=nblk, alpha=alpha),
        grid=(t // tn, nblk + 2),
        in_specs=[
            pl.BlockSpec((tn, d), lambda i, s: (i, 0)),
            row_code, col_code, row_code, col_code,
            pl.BlockSpec((ib * nk // 2, d), lambda i, s: (jnp.minimum(s, last), 0)),
            pl.BlockSpec((d // 2, ib * nk), lambda i, s: (0, jnp.clip(s - 2, 0, last))),
            pl.BlockSpec((1, d), lambda i, s: (0, 0)),
            pl.BlockSpec((1, d), lambda i, s: (0, 0)),
        ],
        out_specs=pl.BlockSpec((tn, d), lambda i, s: (i, 0)),
        out_shape=jax.ShapeDtypeStruct((t, d), F32),
        scratch_shapes=[pltpu.VMEM((d, tn), BF16),
                        pltpu.VMEM((ib * nk, tn), F32), pltpu.VMEM((ib * nk, tn), F32),
                        pltpu.VMEM((ib * nk, tn), BF16), pltpu.VMEM((ib * nk, tn), BF16),
                        pltpu.VMEM((d, tn), F32)],
        compiler_params=_cparams(2),
        name="peer_dense",
    )(x1, na, bm, e1, e2, u_words, vt_words, ln_g, ln_b)


def _pick(n, pref):
    return pref if n % pref == 0 else n


def _trunk_layer(x, conv_buf, gla_state, prm, alpha):
    b, l, d = x.shape
    t = b * l
    xf = x.reshape(t, d)
    tm = _pick(t, 256)

    u, qk, v, g, la = _inproj(xf, prm["w_main"], prm["w_f"], prm["gw_pad"], prm["gate_b"],
                              tm=_pick(t, 512))
    conv_dim = u.shape[-1]

    buf_pad = jnp.pad(conv_buf, ((0, 0), (HALO - (CONV_WIDTH - 1), 0), (0, 0)))
    y_conv, conv_new = _conv_mixer(u.reshape(b, l, conv_dim), buf_pad, prm["conv_w"], prm["conv_b"],
                                   prm["conv_ln_g"], prm["conv_ln_b"], tl=_pick(l, 256))

    c = _pick(l, GLA_CHUNK)
    s0_t = jnp.swapaxes(gla_state, -1, -2)
    y_gla, s_t = _gla_mixer(qk.reshape(b, l, -1), v.reshape(b, l, -1), g.reshape(b, l, -1),
                            la.reshape(b, l, -1), s0_t, prm["gla_norm_g"], c=c)
    gla_new = jnp.swapaxes(s_t, -1, -2)

    x1 = _outproj(xf, y_conv.reshape(t, -1), y_gla.reshape(t, -1), prm["w_out_c"], prm["w_out_g"],
                  prm["ln1_g"], prm["ln1_b"], tm=tm, alpha=alpha)

    na, bm, e1, e2 = _route(x1, prm["wq"], prm["keys"], tm=tm)
    y = _peer_dense(x1, na, bm, e1, e2, prm["u_words"], prm["vt_words"], prm["ln2_g"], prm["ln2_b"],
                    tn=_pick(t, 512), ib=8, alpha=alpha)
    return y.reshape(b, l, d), conv_new, gla_new


def _prep_layer(i, w_in, conv_w, conv_b, conv_ln_g, conv_ln_b, gla_gate_w, gla_gate_b, gla_norm_g,
                w_out, ln1_g, ln1_b, peer_w_query, peer_sub_keys, peer_u, peer_v, ln2_g, ln2_b):
    conv_dim = conv_w.shape[-1]
    half = conv_dim // 2
    wi = w_in[i]
    a_w, g_w, rest = wi[:, :conv_dim], wi[:, conv_dim:2 * conv_dim], wi[:, 2 * conv_dim:]
    n_rest = rest.shape[1] - GLA_RANK
    w_main = jnp.concatenate([a_w[:, :half], g_w[:, :half], a_w[:, half:], g_w[:, half:],
                              rest[:, :n_rest]], axis=1).astype(BF16)
    w_f = jnp.pad(rest[:, n_rest:], ((0, 0), (0, LANES - GLA_RANK))).astype(BF16)
    gw_pad = jnp.pad(gla_gate_w[i], ((0, LANES - GLA_RANK), (0, 0)))
    row = lambda a: a[i][None, :]
    keys = peer_sub_keys[i]
    return dict(
        w_main=w_main, w_f=w_f, gw_pad=gw_pad, gate_b=row(gla_gate_b),
        conv_w=conv_w[i], conv_b=row(conv_b), conv_ln_g=row(conv_ln_g), conv_ln_b=row(conv_ln_b),
        gla_norm_g=row(gla_norm_g),
        w_out_c=w_out[i][:conv_dim].astype(BF16), w_out_g=w_out[i][conv_dim:].astype(BF16),
        ln1_g=row(ln1_g), ln1_b=row(ln1_b),
        wq=peer_w_query[i].astype(BF16),
        keys=keys.reshape(keys.shape[0] * 2, keys.shape[2], keys.shape[3]),
        u_words=_pack_rows(peer_u[i]), vt_words=_pack_rows(peer_v[i].T),
        ln2_g=row(ln2_g), ln2_b=row(ln2_b),
    )


def kernel(x_prompt, x_sample, state_conv, state_gla, w_in, conv_w, conv_b, conv_ln_g, conv_ln_b,
           gla_gate_w, gla_gate_b, gla_norm_g, w_out, ln1_g, ln1_b,
           peer_w_query, peer_sub_keys, peer_u, peer_v, ln2_g, ln2_b):
    depth = w_in.shape[0]
    alpha = (2.0 * depth) ** 0.25
    yp, ys = x_prompt, x_sample
    conv_p, gla_p, conv_s, gla_s = [], [], [], []
    for i in range(depth):
        prm = _prep_layer(i, w_in, conv_w, conv_b, conv_ln_g, conv_ln_b, gla_gate_w, gla_gate_b,
                          gla_norm_g, w_out, ln1_g, ln1_b, peer_w_query, peer_sub_keys,
                          peer_u, peer_v, ln2_g, ln2_b)
        bp = x_prompt.shape[0]
        zero_conv = jnp.zeros((bp,) + state_conv.shape[2:], x_prompt.dtype)
        zero_gla = jnp.zeros((bp,) + state_gla.shape[2:], state_gla.dtype)
        yp, cp, gp = _trunk_layer(yp, zero_conv, zero_gla, prm, alpha)
        ys, cs, gs = _trunk_layer(ys, state_conv[i], state_gla[i], prm, alpha)
        conv_p.append(cp)
        gla_p.append(gp)
        conv_s.append(cs)
        gla_s.append(gs)
    return (yp, ys, jnp.stack(conv_p), jnp.stack(gla_p), jnp.stack(conv_s), jnp.stack(gla_s))
```

```python
import functools

import jax
import jax.numpy as jnp
from jax import lax
from jax.experimental import pallas as pl
from jax.experimental.pallas import tpu as pltpu

F32 = jnp.float32
BF16 = jnp.bfloat16

CONV_WIDTH = 31
GLA_HEADS = 4
GLA_RANK = 16
GLA_TAU = 16.0
PEER_HEADS = 8
PEER_NKEYS = 128
PEER_TOPK = 16
LN_EPS = 1e-5

GLA_CHUNK = 64
GLA_SUB = 16
LANES = 128
SUBLANES = 8
HALO = 32
NOT_LISTED = 8192.0
VMEM_LIMIT = 56 * 1024 * 1024


def _cparams(n_axes):
    return pltpu.CompilerParams(dimension_semantics=("arbitrary",) * n_axes,
                                vmem_limit_bytes=VMEM_LIMIT)


def _layer_norm(x, g, b):
    mu = jnp.mean(x, axis=-1, keepdims=True)
    xc = x - mu
    var = jnp.mean(xc * xc, axis=-1, keepdims=True)
    return xc * lax.rsqrt(var + LN_EPS) * g + b


def _sigmoid(x):
    return 1.0 / (1.0 + jnp.exp(-x))


def _log_sigmoid(x):
    return jnp.minimum(x, 0.0) - jnp.log(1.0 + jnp.exp(-jnp.abs(x)))


def _dot_nt(a, b):
    return lax.dot_general(a, b, (((1,), (1,)), ((), ())), preferred_element_type=F32)


def _dot_tn(a, b):
    return lax.dot_general(a, b, (((0,), (0,)), ((), ())), preferred_element_type=F32)


def _pack_rows(w):
    bits = lax.bitcast_convert_type(w.astype(BF16), jnp.uint16).astype(jnp.uint32)
    return bits[0::2] | (bits[1::2] << 16)


def _pack_rows_of_transpose(w):
    r, c = w.shape
    return lax.bitcast_convert_type(w.astype(BF16).reshape(r, c // 2, 2), jnp.uint32).T


def _unpack_rows(words_ref, row0, nrows):
    return pltpu.bitcast(words_ref[row0 // 2:(row0 + nrows) // 2, :], BF16)


def _split3(x):
    hi = x.astype(BF16)
    r = x - hi.astype(F32)
    mid = r.astype(BF16)
    lo = (r - mid.astype(F32)).astype(BF16)
    return hi, mid, lo


def _inproj_kernel(x_ref, w_ref, wf_ref, gw_ref, gb_ref,
                   u_ref, qk_ref, v_ref, g_ref, la_ref, xb_ref, *, half):
    xb_ref[...] = x_ref[...].astype(BF16)
    blk = 2 * half
    proj = lambda j: jnp.dot(xb_ref[...], w_ref[:, j * blk:(j + 1) * blk], preferred_element_type=F32)
    for j in range(2):
        z = proj(j)
        u_ref[:, j * half:(j + 1) * half] = z[:, :half] * _sigmoid(z[:, half:])
    qk_ref[...] = proj(2).astype(qk_ref.dtype)
    v_ref[...] = proj(3).astype(v_ref.dtype)
    g_ref[...] = proj(4).astype(g_ref.dtype)
    f = jnp.dot(xb_ref[...], wf_ref[...], preferred_element_type=F32)
    pre = jnp.dot(f, gw_ref[...], preferred_element_type=F32,
                  precision=lax.Precision.HIGHEST) + gb_ref[...]
    la_ref[...] = _log_sigmoid(pre) * (1.0 / GLA_TAU)


def _inproj(x, w_main, w_f, gw_pad, gb, *, tm):
    t, d = x.shape
    conv_dim = 1024
    qk2 = 1024
    v_dim = 1024
    assert w_main.shape[1] == 2 * conv_dim + qk2 + 2 * v_dim
    resident = dict(pipeline_mode=pl.Buffered(1))
    out_shapes = (
        jax.ShapeDtypeStruct((t, conv_dim), F32),
        jax.ShapeDtypeStruct((t, qk2), BF16),
        jax.ShapeDtypeStruct((t, v_dim), BF16),
        jax.ShapeDtypeStruct((t, v_dim), BF16),
        jax.ShapeDtypeStruct((t, qk2 // 2), F32),
    )
    row = lambda i: (i, 0)
    fixed = lambda i: (0, 0)
    return pl.pallas_call(
        functools.partial(_inproj_kernel, half=conv_dim // 2),
        grid=(t // tm,),
        in_specs=[
            pl.BlockSpec((tm, d), row),
            pl.BlockSpec(w_main.shape, fixed, **resident),
            pl.BlockSpec((d, LANES), fixed, **resident),
            pl.BlockSpec((LANES, qk2 // 2), fixed, **resident),
            pl.BlockSpec((1, qk2 // 2), fixed, **resident),
        ],
        out_specs=[
            pl.BlockSpec((tm, conv_dim), row),
            pl.BlockSpec((tm, qk2), row),
            pl.BlockSpec((tm, v_dim), row),
            pl.BlockSpec((tm, v_dim), row),
            pl.BlockSpec((tm, qk2 // 2), row),
        ],
        out_shape=out_shapes,
        scratch_shapes=[pltpu.VMEM((tm, d), BF16)],
        compiler_params=_cparams(1),
        name="inproj",
    )(x, w_main, w_f, gw_pad, gb)


def _conv_kernel(buf_ref, u_ref, w_ref, cb_ref, lg_ref, lb_ref, y_ref, st_ref, ext_ref, sh_ref, wb_ref,
                 *, tl, rs):
    l = pl.program_id(1)

    @pl.when(l == 0)
    def _():
        ext_ref[0:HALO, :] = buf_ref[0]

    @pl.when(l > 0)
    def _():
        ext_ref[0:HALO, :] = ext_ref[tl:tl + HALO, :]

    ext_ref[HALO:HALO + tl, :] = u_ref[0]
    first = HALO - (CONV_WIDTH - 1)
    span = sh_ref.shape[1]
    for r in range(1, SUBLANES):
        sh_ref[r - 1] = ext_ref[r:r + span, :]

    def window(row, n):
        shift = row % SUBLANES
        base = row - shift
        return ext_ref[base:base + n, :] if shift == 0 else sh_ref[shift - 1, base:base + n, :]

    @pl.when((pl.program_id(0) == 0) & (l == 0))
    def _():
        for j in range(CONV_WIDTH):
            wb_ref[j] = jnp.broadcast_to(w_ref[j:j + 1, :], wb_ref.shape[1:])

    for r0 in range(0, tl, rs):
        acc = window(first + r0, rs) * wb_ref[0]
        for j in range(1, CONV_WIDTH):
            acc = acc + window(first + r0 + j, rs) * wb_ref[j]
        y = _layer_norm(acc + cb_ref[...], lg_ref[...], lb_ref[...])
        y_ref[0, r0:r0 + rs, :] = (y * _sigmoid(y)).astype(y_ref.dtype)
    st_ref[0] = ext_ref[tl + first:tl + HALO, :]


def _conv_mixer(u, buf_pad, conv_w, conv_b, ln_g, ln_b, *, tl):
    b, l, c = u.shape
    rs = min(16, tl)
    fixed = lambda i, j: (0, 0)
    return pl.pallas_call(
        functools.partial(_conv_kernel, tl=tl, rs=rs),
        grid=(b, l // tl),
        in_specs=[
            pl.BlockSpec((1, HALO, c), lambda i, j: (i, 0, 0)),
            pl.BlockSpec((1, tl, c), lambda i, j: (i, j, 0)),
            pl.BlockSpec((CONV_WIDTH, c), fixed),
            pl.BlockSpec((1, c), fixed),
            pl.BlockSpec((1, c), fixed),
            pl.BlockSpec((1, c), fixed),
        ],
        out_specs=[
            pl.BlockSpec((1, tl, c), lambda i, j: (i, j, 0)),
            pl.BlockSpec((1, CONV_WIDTH - 1, c), lambda i, j: (i, 0, 0)),
        ],
        out_shape=(jax.ShapeDtypeStruct((b, l, c), BF16),
                   jax.ShapeDtypeStruct((b, CONV_WIDTH - 1, c), F32)),
        scratch_shapes=[pltpu.VMEM((HALO + tl, c), F32),
                        pltpu.VMEM((SUBLANES - 1, tl + HALO - SUBLANES, c), F32),
                        pltpu.VMEM((CONV_WIDTH, rs, c), F32)],
        compiler_params=_cparams(2),
        name="conv_mixer",
    )(buf_pad, u, conv_w, conv_b, ln_g, ln_b)


def _gla_kernel(qk_ref, v_ref, g_ref, la_ref, s0_ref, ng_ref, y_ref, sout_ref, st_ref, *, c, dk, dv):
    l = pl.program_id(1)

    @pl.when(l == 0)
    def _():
        st_ref[...] = s0_ref[0]

    hk = GLA_HEADS * dk
    la = la_ref[0]
    rows = lax.broadcasted_iota(jnp.int32, (c, c), 0)
    cols = lax.broadcasted_iota(jnp.int32, (c, c), 1)
    causal = rows >= cols
    tril = jnp.where(causal, 1.0, 0.0).astype(BF16)
    hi, mid, lo = _split3(la)
    bcum = (jnp.dot(tril, hi, preferred_element_type=F32)
            + jnp.dot(tril, mid, preferred_element_type=F32)
            + jnp.dot(tril, lo, preferred_element_type=F32))
    b_last = bcum[c - 1:c, :]
    q = qk_ref[0, :, :hk].astype(F32) * (dk ** -0.5)
    k = qk_ref[0, :, hk:].astype(F32)
    q_in = (q * jnp.exp(bcum)).astype(BF16)
    k_out = (k * jnp.exp(b_last - bcum)).astype(BF16)
    row_id = lax.broadcasted_iota(jnp.int32, (c, dk), 0)

    for h in range(GLA_HEADS):
        ks = slice(h * dk, (h + 1) * dk)
        vs = slice(h * dv, (h + 1) * dv)
        bh = bcum[:, ks]
        qh = q[:, ks]
        kh = k[:, ks]
        vh = v_ref[0, :, vs].astype(BF16)
        st = st_ref[h]
        o = _dot_nt(q_in[:, ks], st.astype(BF16))
        blocks = []
        for i in range(c // GLA_SUB):
            r0 = i * GLA_SUB
            base = bh[r0 - 1:r0, :] if i > 0 else jnp.zeros((1, dk), F32)
            q_t = qh[r0:r0 + GLA_SUB, :] * jnp.exp(bh[r0:r0 + GLA_SUB, :] - base)
            k_t = kh * jnp.exp(jnp.where(row_id < r0 + GLA_SUB, base - bh, -1e30))
            blocks.append(_dot_nt(q_t.astype(BF16), k_t.astype(BF16)))
        scores = jnp.where(causal, jnp.concatenate(blocks, axis=0), 0.0)
        o = o + jnp.dot(scores.astype(BF16), vh, preferred_element_type=F32)
        st_ref[h] = st * jnp.exp(b_last[:, ks]) + _dot_tn(vh, k_out[:, ks])
        o = o * lax.rsqrt(jnp.mean(o * o, axis=-1, keepdims=True) + LN_EPS)
        gh = g_ref[0, :, vs].astype(F32)
        y_ref[0, :, vs] = (o * ng_ref[:, vs] * (gh * _sigmoid(gh))).astype(y_ref.dtype)

    sout_ref[0] = st_ref[...]


def _gla_mixer(qk, v, g, la, s0_t, norm_g, *, c):
    b, l, hk2 = qk.shape
    dk = hk2 // (2 * GLA_HEADS)
    dv = v.shape[-1] // GLA_HEADS
    tile = lambda i, j: (i, j, 0)
    return pl.pallas_call(
        functools.partial(_gla_kernel, c=c, dk=dk, dv=dv),
        grid=(b, l // c),
        in_specs=[
            pl.BlockSpec((1, c, hk2), tile),
            pl.BlockSpec((1, c, GLA_HEADS * dv), tile),
            pl.BlockSpec((1, c, GLA_HEADS * dv), tile),
            pl.BlockSpec((1, c, hk2 // 2), tile),
            pl.BlockSpec((1, GLA_HEADS, dv, dk), lambda i, j: (i, 0, 0, 0)),
            pl.BlockSpec((1, GLA_HEADS * dv), lambda i, j: (0, 0)),
        ],
        out_specs=[
            pl.BlockSpec((1, c, GLA_HEADS * dv), tile),
            pl.BlockSpec((1, GLA_HEADS, dv, dk), lambda i, j: (i, 0, 0, 0)),
        ],
        out_shape=(jax.ShapeDtypeStruct((b, l, GLA_HEADS * dv), BF16),
                   jax.ShapeDtypeStruct((b, GLA_HEADS, dv, dk), F32)),
        scratch_shapes=[pltpu.VMEM((GLA_HEADS, dv, dk), F32)],
        compiler_params=_cparams(2),
        name="gla_mixer",
    )(qk, v, g, la, s0_t, norm_g)


def _outproj_kernel(x_ref, yc_ref, yg_ref, wc_ref, wg_ref, g_ref, b_ref, o_ref, *, alpha):
    mix = (jnp.dot(yc_ref[...].astype(BF16), wc_ref[...], preferred_element_type=F32)
           + jnp.dot(yg_ref[...].astype(BF16), wg_ref[...], preferred_element_type=F32))
    o_ref[...] = _layer_norm(alpha * x_ref[...] + mix, g_ref[...], b_ref[...])


def _outproj(x, y_conv, y_gla, w_c, w_g, ln_g, ln_b, *, tm, alpha):
    t, d = x.shape
    row = lambda i: (i, 0)
    fixed = lambda i: (0, 0)
    return pl.pallas_call(
        functools.partial(_outproj_kernel, alpha=alpha),
        grid=(t // tm,),
        in_specs=[
            pl.BlockSpec((tm, d), row),
            pl.BlockSpec((tm, y_conv.shape[1]), row),
            pl.BlockSpec((tm, y_gla.shape[1]), row),
            pl.BlockSpec(w_c.shape, fixed),
            pl.BlockSpec(w_g.shape, fixed),
            pl.BlockSpec((1, d), fixed),
            pl.BlockSpec((1, d), fixed),
        ],
        out_specs=pl.BlockSpec((tm, d), row),
        out_shape=jax.ShapeDtypeStruct((t, d), F32),
        compiler_params=_cparams(1),
        name="outproj_ln",
    )(x, y_conv, y_gla, w_c, w_g, ln_g, ln_b)


def _extract_topk(s):
    n = s.shape[0]
    rows = lax.broadcasted_iota(jnp.int32, s.shape, 0).astype(F32)
    rank = jnp.full(s.shape, NOT_LISTED, F32)
    vals = []
    for r in range(PEER_TOPK):
        m = jnp.max(s, axis=0, keepdims=True)
        first = jnp.min(jnp.where(s == m, rows, float(n)), axis=0, keepdims=True)
        hit = rows == first
        rank = jnp.where(hit, float(r), rank)
        s = jnp.where(hit, -jnp.inf, s)
        vals.append(m)
    return vals, rank


def _sort_pairs(lo, hi):
    def merge(lo, hi, r):
        step = r * 2
        if step < hi - lo:
            yield from merge(lo, hi, step)
            yield from merge(lo + r, hi, step)
            yield from [(i, i + r) for i in range(lo + r, hi - r, step)]
        else:
            yield (lo, lo + r)

    if hi - lo >= 1:
        mid = lo + (hi - lo) // 2
        yield from _sort_pairs(lo, mid)
        yield from _sort_pairs(mid + 1, hi)
        yield from merge(lo, hi, 1)


def _topk_values(s):
    k = PEER_TOPK
    v = [s[r:r + SUBLANES, :] for r in range(0, s.shape[0], SUBLANES)]
    assert len(v) >= k and len(v) & (len(v) - 1) == 0
    for i, j in _sort_pairs(0, len(v) - 1):
        v[i], v[j] = jnp.maximum(v[i], v[j]), jnp.minimum(v[i], v[j])
    vals = []
    taken = jnp.zeros(v[0].shape, F32)
    for t in range(k):
        m = jnp.max(v[0], axis=0, keepdims=True)
        hit = v[0] == m
        taken = taken + jnp.where(hit, 1.0, 0.0)
        for r in range(k - 1 - t):
            v[r] = jnp.where(hit, v[r + 1], v[r])
        vals.append(m)
    equal = jnp.sum(taken, axis=0, keepdims=True) != float(k)
    for t in range(k - 1):
        equal = equal | (vals[t] == vals[t + 1])
    at_least = jnp.sum(jnp.where(s >= vals[k - 1], 1.0, 0.0), axis=0, keepdims=True)
    return vals, equal | (at_least != float(k))


def _dup16(x):
    bits = lax.bitcast_convert_type(x.astype(BF16).astype(F32), jnp.uint32)
    return lax.bitcast_convert_type(bits | (bits >> 16), jnp.int32)


def _route_head(s1, s2, exact):
    k = PEER_TOPK
    assert k == 16
    if exact:
        vals1, rank1 = _extract_topk(s1)
        vals2, rank2 = _extract_topk(s2)
        unusable = None
    else:
        vals1, equal1 = _topk_values(s1)
        vals2, equal2 = _topk_values(s2)
        unusable = equal1 | equal2
    v2 = jnp.concatenate(vals2, axis=0)
    v1_tail = jnp.concatenate(vals1[8:], axis=0)
    ev2 = jnp.exp(v2 - vals2[0])
    ev1 = [jnp.exp(vals1[a] - vals1[0]) for a in range(8)]
    ev1_tail = jnp.exp(v1_tail - vals1[0])
    cand = jnp.concatenate([vals1[0] + v2] + [vals1[a] + v2[0:8] for a in range(1, 8)]
                           + [v1_tail + vals2[0]], axis=0)
    gate = jnp.concatenate([ev1[0] * ev2] + [ev1[a] * ev2[0:8] for a in range(1, 8)]
                           + [ev1_tail * ev2[0:1]], axis=0)
    r = lax.broadcasted_iota(jnp.int32, cand.shape, 0)
    a_mid = 1 + ((r - 16) >> 3)
    b_mid = (r - 16) & 7
    head_rows = r < 16
    tail_rows = r >= 72
    flat = jnp.where(head_rows, r, jnp.where(tail_rows, (r - 64) * k, a_mid * k + b_mid)).astype(F32)
    valid = head_rows | tail_rows | ((a_mid + 1) * (b_mid + 1) <= k)
    cand = jnp.where(valid, cand, -jnp.inf)
    sel = jnp.zeros(cand.shape, F32)
    for _ in range(k):
        m = jnp.max(cand, axis=0, keepdims=True)
        if exact:
            first = jnp.min(jnp.where(cand == m, flat, float(k * k)), axis=0, keepdims=True)
            hit = flat == first
        else:
            hit = cand == m
        sel = jnp.where(hit, 1.0, sel)
        cand = jnp.where(hit, -jnp.inf, cand)
    z = jnp.sum(sel * gate, axis=0, keepdims=True)
    heights = [jnp.sum(sel[0:16], axis=0, keepdims=True)]
    heights += [jnp.sum(sel[8 + 8 * a:16 + 8 * a], axis=0, keepdims=True) for a in range(1, 8)]
    heights += [sel[64 + a:65 + a] for a in range(8, 16)]
    na = jnp.full(s1.shape, NOT_LISTED, F32)
    bm = jnp.full(s2.shape, -2.0 * NOT_LISTED, F32)
    for a in range(k):
        na = jnp.where((rank1 == float(a)) if exact else (s1 == vals1[a]), 1.0 - heights[a], na)
        bm = jnp.where((rank2 == float(a)) if exact else (s2 == vals2[a]), -float(a), bm)
    e1 = jnp.exp(s1 - vals1[0]) / z
    e2 = jnp.exp(s2 - vals2[0])
    if not exact:
        unusable = unusable | (jnp.sum(sel, axis=0, keepdims=True) != float(k))
    return na, bm, e1, e2, unusable


def _route_kernel(x_ref, wq_ref, keys_ref, na_ref, bm_ref, e1_ref, e2_ref, s_ref):
    xq = jnp.dot(x_ref[...].astype(BF16), wq_ref[...], preferred_element_type=F32)
    dsub = keys_ref.shape[-1]
    for h in range(PEER_HEADS):
        for p in range(2):
            off = (h * 2 + p) * dsub
            qh, qm, ql = _split3(xq[:, off:off + dsub])
            kh, km, kl = _split3(keys_ref[h * 2 + p])
            s_ref[p] = (_dot_nt(kh, qh) + (_dot_nt(kh, qm) + _dot_nt(km, qh))
                        + (_dot_nt(kh, ql) + _dot_nt(km, qm) + _dot_nt(kl, qh)))

        def emit(exact, h=h):
            na, bm, e1, e2, unusable = _route_head(s_ref[0], s_ref[1], exact)
            na_ref[h] = _dup16(na)
            bm_ref[h] = bm.astype(BF16)
            e1_ref[h] = _dup16(e1)
            e2_ref[h] = e2.astype(BF16)
            return unusable

        unusable = emit(False)

        @pl.when(jnp.sum(jnp.where(unusable, 1.0, 0.0)) > 0.0)
        def _():
            emit(True)


def _route(x1, wq, keys, *, tm):
    t, d = x1.shape
    nk = keys.shape[1]
    ospec = pl.BlockSpec((PEER_HEADS, nk, tm), lambda i: (0, 0, i))
    words = jax.ShapeDtypeStruct((PEER_HEADS, nk, t), jnp.int32)
    halfs = jax.ShapeDtypeStruct((PEER_HEADS, nk, t), BF16)
    return pl.pallas_call(
        _route_kernel,
        grid=(t // tm,),
        in_specs=[
            pl.BlockSpec((tm, d), lambda i: (i, 0)),
            pl.BlockSpec(wq.shape, lambda i: (0, 0)),
            pl.BlockSpec(keys.shape, lambda i: (0, 0, 0)),
        ],
        out_specs=[ospec] * 4,
        out_shape=(words, halfs, words, halfs),
        scratch_shapes=[pltpu.VMEM((2, nk, tm), F32)],
        compiler_params=_cparams(1),
        name="peer_route",
    )(x1, wq, keys)


def _gelu(x):
    return 0.5 * x * (1.0 + lax.erf(x * 0.7071067811865476))


def _peer_kernel(x_ref, na_ref, bm_ref, e1_ref, e2_ref, u_ref, vt_ref, g_ref, b_ref, y_ref,
                 xt_ref, h0_ref, h1_ref, p0_ref, p1_ref, acc_ref, *, ib, nblk, alpha):
    step = pl.program_id(1)
    nk = bm_ref.shape[1]
    tn = xt_ref.shape[1]

    def stage(h_new, h_old, p_new, p_old, project, weigh, combine):
        parts = 4
        group = 16
        hm = ib * nk // parts
        dm = acc_ref.shape[0] // parts
        for q in range(parts):
            if project:
                h_new[q * hm:(q + 1) * hm, :] = jnp.dot(_unpack_rows(u_ref, q * hm, hm), xt_ref[...],
                                                        preferred_element_type=F32)
            if weigh:
                for ii in range(q * ib // parts, (q + 1) * ib // parts):
                    row = lambda ref, h: pltpu.bitcast(
                        jnp.broadcast_to(ref[h, ii:ii + 1, :], (group // 2, tn)), BF16)
                    na_b = [row(na_ref, h) for h in range(PEER_HEADS)]
                    e1_b = [row(e1_ref, h) for h in range(PEER_HEADS)]
                    for r0 in range(0, nk, group):
                        cols = slice(r0, r0 + group)
                        w = jnp.zeros((group, tn), BF16)
                        for h in range(PEER_HEADS):
                            w = w + jnp.where(bm_ref[h, cols, :] >= na_b[h], e2_ref[h, cols, :] * e1_b[h],
                                              jnp.zeros((group, tn), BF16))
                        rows = slice(ii * nk + r0, ii * nk + r0 + group)
                        p_new[rows, :] = _gelu(h_old[rows, :]).astype(BF16) * w
            if combine:
                acc_ref[q * dm:(q + 1) * dm, :] += jnp.dot(_unpack_rows(vt_ref, q * dm, dm), p_old[...],
                                                           preferred_element_type=F32)

    @pl.when(step == 0)
    def _():
        for c0 in range(0, tn, LANES):
            xt_ref[:, c0:c0 + LANES] = x_ref[c0:c0 + LANES, :].T.astype(BF16)
        acc_ref[...] = jnp.zeros_like(acc_ref)
        stage(h0_ref, None, None, None, True, False, False)

    @pl.when(step == 1)
    def _():
        stage(h1_ref, h0_ref, p1_ref, None, True, True, False)

    mid = (step >= 2) & (step < nblk)

    @pl.when(mid & (step % 2 == 0))
    def _():
        stage(h0_ref, h1_ref, p0_ref, p1_ref, True, True, True)

    @pl.when(mid & (step % 2 == 1))
    def _():
        stage(h1_ref, h0_ref, p1_ref, p0_ref, True, True, True)

    @pl.when(step == nblk)
    def _():
        stage(None, h1_ref, p0_ref, p1_ref, False, True, True)

    @pl.when(step == nblk + 1)
    def _():
        stage(None, None, None, p0_ref, False, False, True)
        for c0 in range(0, tn, LANES):
            rows = slice(c0, c0 + LANES)
            y_ref[rows, :] = _layer_norm(alpha * x_ref[rows, :] + acc_ref[:, rows].T, g_ref[...], b_ref[...])


def _peer_dense(x1, na, bm, e1, e2, u_words, vt_words, ln_g, ln_b, *, tn, ib, alpha):
    t, d = x1.shape
    nk = bm.shape[1]
    nblk = 2 * u_words.shape[0] // (ib * nk)
    assert nblk % 2 == 0 and nblk >= 4
    last = nblk - 1
    weigh_blk = lambda s: jnp.clip(s - 1, 0, last)
    row_code = pl.BlockSpec((PEER_HEADS, ib, tn), lambda i, s: (0, weigh_blk(s), i))
    col_code = pl.BlockSpec((PEER_HEADS, nk, tn), lambda i, s: (0, 0, i))
    return pl.pallas_call(
        functools.partial(_peer_kernel, ib=ib, nblk=nblk, alpha=alpha),
        grid=(t // tn, nblk + 2),
        in_specs=[
            pl.BlockSpec((tn, d), lambda i, s: (i, 0)),
            row_code, col_code, row_code, col_code,
            pl.BlockSpec((ib * nk // 2, d), lambda i, s: (jnp.minimum(s, last), 0)),
            pl.BlockSpec((d // 2, ib * nk), lambda i, s: (0, jnp.clip(s - 2, 0, last))),
            pl.BlockSpec((1, d), lambda i, s: (0, 0)),
            pl.BlockSpec((1, d), lambda i, s: (0, 0)),
        ],
        out_specs=pl.BlockSpec((tn, d), lambda i, s: (i, 0)),
        out_shape=jax.ShapeDtypeStruct((t, d), F32),
        scratch_shapes=[pltpu.VMEM((d, tn), BF16),
                        pltpu.VMEM((ib * nk, tn), F32), pltpu.VMEM((ib * nk, tn), F32),
                        pltpu.VMEM((ib * nk, tn), BF16), pltpu.VMEM((ib * nk, tn), BF16),
                        pltpu.VMEM((d, tn), F32)],
        compiler_params=_cparams(2),
        name="peer_dense",
    )(x1, na, bm, e1, e2, u_words, vt_words, ln_g, ln_b)


def _pick(n, pref):
    return pref if n % pref == 0 else n


def _trunk_layer(x, conv_buf, gla_state, prm, alpha):
    b, l, d = x.shape
    t = b * l
    xf = x.reshape(t, d)
    tm = _pick(t, 256)

    u, qk, v, g, la = _inproj(xf, prm["w_main"], prm["w_f"], prm["gw_pad"], prm["gate_b"],
                              tm=_pick(t, 512))
    conv_dim = u.shape[-1]

    buf_pad = jnp.pad(conv_buf, ((0, 0), (HALO - (CONV_WIDTH - 1), 0), (0, 0)))
    y_conv, conv_new = _conv_mixer(u.reshape(b, l, conv_dim), buf_pad, prm["conv_w"], prm["conv_b"],
                                   prm["conv_ln_g"], prm["conv_ln_b"], tl=_pick(l, 256))

    c = _pick(l, GLA_CHUNK)
    s0_t = jnp.swapaxes(gla_state, -1, -2)
    y_gla, s_t = _gla_mixer(qk.reshape(b, l, -1), v.reshape(b, l, -1), g.reshape(b, l, -1),
                            la.reshape(b, l, -1), s0_t, prm["gla_norm_g"], c=c)
    gla_new = jnp.swapaxes(s_t, -1, -2)

    x1 = _outproj(xf, y_conv.reshape(t, -1), y_gla.reshape(t, -1), prm["w_out_c"], prm["w_out_g"],
                  prm["ln1_g"], prm["ln1_b"], tm=tm, alpha=alpha)

    na, bm, e1, e2 = _route(x1, prm["wq"], prm["keys"], tm=tm)
    y = _peer_dense(x1, na, bm, e1, e2, prm["u_words"], prm["vt_words"], prm["ln2_g"], prm["ln2_b"],
                    tn=_pick(t, 512), ib=8, alpha=alpha)
    return y.reshape(b, l, d), conv_new, gla_new


def _prep_layer(i, w_in, conv_w, conv_b, conv_ln_g, conv_ln_b, gla_gate_w, gla_gate_b, gla_norm_g,
                w_out, ln1_g, ln1_b, peer_w_query, peer_sub_keys, peer_u, peer_v, ln2_g, ln2_b):
    conv_dim = conv_w.shape[-1]
    half = conv_dim // 2
    wi = w_in[i]
    a_w, g_w, rest = wi[:, :conv_dim], wi[:, conv_dim:2 * conv_dim], wi[:, 2 * conv_dim:]
    n_rest = rest.shape[1] - GLA_RANK
    w_main = jnp.concatenate([a_w[:, :half], g_w[:, :half], a_w[:, half:], g_w[:, half:],
                              rest[:, :n_rest]], axis=1).astype(BF16)
    w_f = jnp.pad(rest[:, n_rest:], ((0, 0), (0, LANES - GLA_RANK))).astype(BF16)
    gw_pad = jnp.pad(gla_gate_w[i], ((0, LANES - GLA_RANK), (0, 0)))
    row = lambda a: a[i][None, :]
    keys = peer_sub_keys[i]
    return dict(
        w_main=w_main, w_f=w_f, gw_pad=gw_pad, gate_b=row(gla_gate_b),
        conv_w=conv_w[i], conv_b=row(conv_b), conv_ln_g=row(conv_ln_g), conv_ln_b=row(conv_ln_b),
        gla_norm_g=row(gla_norm_g),
        w_out_c=w_out[i][:conv_dim].astype(BF16), w_out_g=w_out[i][conv_dim:].astype(BF16),
        ln1_g=row(ln1_g), ln1_b=row(ln1_b),
        wq=peer_w_query[i].astype(BF16),
        keys=keys.reshape(keys.shape[0] * 2, keys.shape[2], keys.shape[3]),
        u_words=_pack_rows(peer_u[i]), vt_words=_pack_rows_of_transpose(peer_v[i]),
        ln2_g=row(ln2_g), ln2_b=row(ln2_b),
    )


def kernel(x_prompt, x_sample, state_conv, state_gla, w_in, conv_w, conv_b, conv_ln_g, conv_ln_b,
           gla_gate_w, gla_gate_b, gla_norm_g, w_out, ln1_g, ln1_b,
           peer_w_query, peer_sub_keys, peer_u, peer_v, ln2_g, ln2_b):
    depth = w_in.shape[0]
    alpha = (2.0 * depth) ** 0.25
    yp, ys = x_prompt, x_sample
    conv_p, gla_p, conv_s, gla_s = [], [], [], []
    for i in range(depth):
        prm = _prep_layer(i, w_in, conv_w, conv_b, conv_ln_g, conv_ln_b, gla_gate_w, gla_gate_b,
                          gla_norm_g, w_out, ln1_g, ln1_b, peer_w_query, peer_sub_keys,
                          peer_u, peer_v, ln2_g, ln2_b)
        bp = x_prompt.shape[0]
        zero_conv = jnp.zeros((bp,) + state_conv.shape[2:], x_prompt.dtype)
        zero_gla = jnp.zeros((bp,) + state_gla.shape[2:], state_gla.dtype)
        yp, cp, gp = _trunk_layer(yp, zero_conv, zero_gla, prm, alpha)
        ys, cs, gs = _trunk_layer(ys, state_conv[i], state_gla[i], prm, alpha)
        conv_p.append(cp)
        gla_p.append(gp)
        conv_s.append(cs)
        gla_s.append(gs)
    return (yp, ys, jnp.stack(conv_p), jnp.stack(gla_p), jnp.stack(conv_s), jnp.stack(gla_s))
```

```python
import functools

import jax
import jax.numpy as jnp
from jax import lax
from jax.experimental import pallas as pl
from jax.experimental.pallas import tpu as pltpu

F32 = jnp.float32
BF16 = jnp.bfloat16

CONV_WIDTH = 31
GLA_HEADS = 4
GLA_RANK = 16
GLA_TAU = 16.0
PEER_HEADS = 8
PEER_NKEYS = 128
PEER_TOPK = 16
LN_EPS = 1e-5

GLA_CHUNK = 64
GLA_SUB = 16
LANES = 128
SUBLANES = 8
HALO = 32
NOT_LISTED = 8192.0
VMEM_LIMIT = 56 * 1024 * 1024


def _cparams(n_axes):
    return pltpu.CompilerParams(dimension_semantics=("arbitrary",) * n_axes,
                                vmem_limit_bytes=VMEM_LIMIT)


def _layer_norm(x, g, b):
    mu = jnp.mean(x, axis=-1, keepdims=True)
    xc = x - mu
    var = jnp.mean(xc * xc, axis=-1, keepdims=True)
    return xc * lax.rsqrt(var + LN_EPS) * g + b


def _sigmoid(x):
    return 1.0 / (1.0 + jnp.exp(-x))


def _log_sigmoid(x):
    return jnp.minimum(x, 0.0) - jnp.log(1.0 + jnp.exp(-jnp.abs(x)))


def _dot_nt(a, b):
    return lax.dot_general(a, b, (((1,), (1,)), ((), ())), preferred_element_type=F32)


def _dot_tn(a, b):
    return lax.dot_general(a, b, (((0,), (0,)), ((), ())), preferred_element_type=F32)


def _pack_kernel(w_ref, o_ref, *, transpose):
    w = w_ref[...]
    o_ref[...] = pltpu.bitcast((w.T if transpose else w).astype(BF16), jnp.uint32)


def _pack_table(w, *, transpose, rows=512):
    r, c = w.shape
    if transpose:
        out_shape, out_block, out_index = (c // 2, r), (c // 2, rows), lambda i: (0, i)
    else:
        out_shape, out_block, out_index = (r // 2, c), (rows // 2, c), lambda i: (i, 0)
    return pl.pallas_call(
        functools.partial(_pack_kernel, transpose=transpose),
        grid=(r // rows,),
        in_specs=[pl.BlockSpec((rows, c), lambda i: (i, 0))],
        out_specs=pl.BlockSpec(out_block, out_index),
        out_shape=jax.ShapeDtypeStruct(out_shape, jnp.uint32),
        compiler_params=_cparams(1),
        name="pack_table",
    )(w)


def _unpack_rows(words_ref, row0, nrows):
    return pltpu.bitcast(words_ref[row0 // 2:(row0 + nrows) // 2, :], BF16)


def _split3(x):
    hi = x.astype(BF16)
    r = x - hi.astype(F32)
    mid = r.astype(BF16)
    lo = (r - mid.astype(F32)).astype(BF16)
    return hi, mid, lo


def _inproj_kernel(x_ref, w_ref, wf_ref, gw_ref, gb_ref,
                   u_ref, qk_ref, v_ref, g_ref, la_ref, xb_ref, *, half):
    xb_ref[...] = x_ref[...].astype(BF16)
    blk = 2 * half
    proj = lambda j: jnp.dot(xb_ref[...], w_ref[:, j * blk:(j + 1) * blk], preferred_element_type=F32)
    for j in range(2):
        z = proj(j)
        u_ref[:, j * half:(j + 1) * half] = z[:, :half] * _sigmoid(z[:, half:])
    qk_ref[...] = proj(2).astype(qk_ref.dtype)
    v_ref[...] = proj(3).astype(v_ref.dtype)
    g_ref[...] = proj(4).astype(g_ref.dtype)
    f = jnp.dot(xb_ref[...], wf_ref[...], preferred_element_type=F32)
    pre = jnp.dot(f, gw_ref[...], preferred_element_type=F32,
                  precision=lax.Precision.HIGHEST) + gb_ref[...]
    la_ref[...] = _log_sigmoid(pre) * (1.0 / GLA_TAU)


def _inproj(x, w_main, w_f, gw_pad, gb, *, tm):
    t, d = x.shape
    conv_dim = 1024
    qk2 = 1024
    v_dim = 1024
    assert w_main.shape[1] == 2 * conv_dim + qk2 + 2 * v_dim
    resident = dict(pipeline_mode=pl.Buffered(1))
    out_shapes = (
        jax.ShapeDtypeStruct((t, conv_dim), F32),
        jax.ShapeDtypeStruct((t, qk2), BF16),
        jax.ShapeDtypeStruct((t, v_dim), BF16),
        jax.ShapeDtypeStruct((t, v_dim), BF16),
        jax.ShapeDtypeStruct((t, qk2 // 2), F32),
    )
    row = lambda i: (i, 0)
    fixed = lambda i: (0, 0)
    return pl.pallas_call(
        functools.partial(_inproj_kernel, half=conv_dim // 2),
        grid=(t // tm,),
        in_specs=[
            pl.BlockSpec((tm, d), row),
            pl.BlockSpec(w_main.shape, fixed, **resident),
            pl.BlockSpec((d, LANES), fixed, **resident),
            pl.BlockSpec((LANES, qk2 // 2), fixed, **resident),
            pl.BlockSpec((1, qk2 // 2), fixed, **resident),
        ],
        out_specs=[
            pl.BlockSpec((tm, conv_dim), row),
            pl.BlockSpec((tm, qk2), row),
            pl.BlockSpec((tm, v_dim), row),
            pl.BlockSpec((tm, v_dim), row),
            pl.BlockSpec((tm, qk2 // 2), row),
        ],
        out_shape=out_shapes,
        scratch_shapes=[pltpu.VMEM((tm, d), BF16)],
        compiler_params=_cparams(1),
        name="inproj",
    )(x, w_main, w_f, gw_pad, gb)


def _conv_kernel(buf_ref, u_ref, w_ref, cb_ref, lg_ref, lb_ref, y_ref, st_ref, ext_ref, sh_ref, wb_ref,
                 *, tl, rs):
    l = pl.program_id(1)

    @pl.when(l == 0)
    def _():
        ext_ref[0:HALO, :] = buf_ref[0]

    @pl.when(l > 0)
    def _():
        ext_ref[0:HALO, :] = ext_ref[tl:tl + HALO, :]

    ext_ref[HALO:HALO + tl, :] = u_ref[0]
    first = HALO - (CONV_WIDTH - 1)
    span = sh_ref.shape[1]
    for r in range(1, SUBLANES):
        sh_ref[r - 1] = ext_ref[r:r + span, :]

    def window(row, n):
        shift = row % SUBLANES
        base = row - shift
        return ext_ref[base:base + n, :] if shift == 0 else sh_ref[shift - 1, base:base + n, :]

    @pl.when((pl.program_id(0) == 0) & (l == 0))
    def _():
        for j in range(CONV_WIDTH):
            wb_ref[j] = jnp.broadcast_to(w_ref[j:j + 1, :], wb_ref.shape[1:])

    for r0 in range(0, tl, rs):
        acc = window(first + r0, rs) * wb_ref[0]
        for j in range(1, CONV_WIDTH):
            acc = acc + window(first + r0 + j, rs) * wb_ref[j]
        y = _layer_norm(acc + cb_ref[...], lg_ref[...], lb_ref[...])
        y_ref[0, r0:r0 + rs, :] = (y * _sigmoid(y)).astype(y_ref.dtype)
    st_ref[0] = ext_ref[tl + first:tl + HALO, :]


def _conv_mixer(u, buf_pad, conv_w, conv_b, ln_g, ln_b, *, tl):
    b, l, c = u.shape
    rs = min(16, tl)
    fixed = lambda i, j: (0, 0)
    return pl.pallas_call(
        functools.partial(_conv_kernel, tl=tl, rs=rs),
        grid=(b, l // tl),
        in_specs=[
            pl.BlockSpec((1, HALO, c), lambda i, j: (i, 0, 0)),
            pl.BlockSpec((1, tl, c), lambda i, j: (i, j, 0)),
            pl.BlockSpec((CONV_WIDTH, c), fixed),
            pl.BlockSpec((1, c), fixed),
            pl.BlockSpec((1, c), fixed),
            pl.BlockSpec((1, c), fixed),
        ],
        out_specs=[
            pl.BlockSpec((1, tl, c), lambda i, j: (i, j, 0)),
            pl.BlockSpec((1, CONV_WIDTH - 1, c), lambda i, j: (i, 0, 0)),
        ],
        out_shape=(jax.ShapeDtypeStruct((b, l, c), BF16),
                   jax.ShapeDtypeStruct((b, CONV_WIDTH - 1, c), F32)),
        scratch_shapes=[pltpu.VMEM((HALO + tl, c), F32),
                        pltpu.VMEM((SUBLANES - 1, tl + HALO - SUBLANES, c), F32),
                        pltpu.VMEM((CONV_WIDTH, rs, c), F32)],
        compiler_params=_cparams(2),
        name="conv_mixer",
    )(buf_pad, u, conv_w, conv_b, ln_g, ln_b)


def _gla_kernel(qk_ref, v_ref, g_ref, la_ref, s0_ref, ng_ref, y_ref, sout_ref, st_ref, *, c, dk, dv):
    l = pl.program_id(1)

    @pl.when(l == 0)
    def _():
        st_ref[...] = s0_ref[0]

    hk = GLA_HEADS * dk
    la = la_ref[0]
    rows = lax.broadcasted_iota(jnp.int32, (c, c), 0)
    cols = lax.broadcasted_iota(jnp.int32, (c, c), 1)
    causal = rows >= cols
    tril = jnp.where(causal, 1.0, 0.0).astype(BF16)
    hi, mid, lo = _split3(la)
    bcum = (jnp.dot(tril, hi, preferred_element_type=F32)
            + jnp.dot(tril, mid, preferred_element_type=F32)
            + jnp.dot(tril, lo, preferred_element_type=F32))
    b_last = bcum[c - 1:c, :]
    q = qk_ref[0, :, :hk].astype(F32) * (dk ** -0.5)
    k = qk_ref[0, :, hk:].astype(F32)
    q_in = (q * jnp.exp(bcum)).astype(BF16)
    k_out = (k * jnp.exp(b_last - bcum)).astype(BF16)
    row_id = lax.broadcasted_iota(jnp.int32, (c, dk), 0)

    for h in range(GLA_HEADS):
        ks = slice(h * dk, (h + 1) * dk)
        vs = slice(h * dv, (h + 1) * dv)
        bh = bcum[:, ks]
        qh = q[:, ks]
        kh = k[:, ks]
        vh = v_ref[0, :, vs].astype(BF16)
        st = st_ref[h]
        o = _dot_nt(q_in[:, ks], st.astype(BF16))
        blocks = []
        for i in range(c // GLA_SUB):
            r0 = i * GLA_SUB
            base = bh[r0 - 1:r0, :] if i > 0 else jnp.zeros((1, dk), F32)
            q_t = qh[r0:r0 + GLA_SUB, :] * jnp.exp(bh[r0:r0 + GLA_SUB, :] - base)
            k_t = kh * jnp.exp(jnp.where(row_id < r0 + GLA_SUB, base - bh, -1e30))
            blocks.append(_dot_nt(q_t.astype(BF16), k_t.astype(BF16)))
        scores = jnp.where(causal, jnp.concatenate(blocks, axis=0), 0.0)
        o = o + jnp.dot(scores.astype(BF16), vh, preferred_element_type=F32)
        st_ref[h] = st * jnp.exp(b_last[:, ks]) + _dot_tn(vh, k_out[:, ks])
        o = o * lax.rsqrt(jnp.mean(o * o, axis=-1, keepdims=True) + LN_EPS)
        gh = g_ref[0, :, vs].astype(F32)
        y_ref[0, :, vs] = (o * ng_ref[:, vs] * (gh * _sigmoid(gh))).astype(y_ref.dtype)

    sout_ref[0] = st_ref[...]


def _gla_mixer(qk, v, g, la, s0_t, norm_g, *, c):
    b, l, hk2 = qk.shape
    dk = hk2 // (2 * GLA_HEADS)
    dv = v.shape[-1] // GLA_HEADS
    tile = lambda i, j: (i, j, 0)
    return pl.pallas_call(
        functools.partial(_gla_kernel, c=c, dk=dk, dv=dv),
        grid=(b, l // c),
        in_specs=[
            pl.BlockSpec((1, c, hk2), tile),
            pl.BlockSpec((1, c, GLA_HEADS * dv), tile),
            pl.BlockSpec((1, c, GLA_HEADS * dv), tile),
            pl.BlockSpec((1, c, hk2 // 2), tile),
            pl.BlockSpec((1, GLA_HEADS, dv, dk), lambda i, j: (i, 0, 0, 0)),
            pl.BlockSpec((1, GLA_HEADS * dv), lambda i, j: (0, 0)),
        ],
        out_specs=[
            pl.BlockSpec((1, c, GLA_HEADS * dv), tile),
            pl.BlockSpec((1, GLA_HEADS, dv, dk), lambda i, j: (i, 0, 0, 0)),
        ],
        out_shape=(jax.ShapeDtypeStruct((b, l, GLA_HEADS * dv), BF16),
                   jax.ShapeDtypeStruct((b, GLA_HEADS, dv, dk), F32)),
        scratch_shapes=[pltpu.VMEM((GLA_HEADS, dv, dk), F32)],
        compiler_params=_cparams(2),
        name="gla_mixer",
    )(qk, v, g, la, s0_t, norm_g)


def _outproj_kernel(x_ref, yc_ref, yg_ref, wc_ref, wg_ref, g_ref, b_ref, o_ref, *, alpha):
    mix = (jnp.dot(yc_ref[...].astype(BF16), wc_ref[...], preferred_element_type=F32)
           + jnp.dot(yg_ref[...].astype(BF16), wg_ref[...], preferred_element_type=F32))
    o_ref[...] = _layer_norm(alpha * x_ref[...] + mix, g_ref[...], b_ref[...])


def _outproj(x, y_conv, y_gla, w_c, w_g, ln_g, ln_b, *, tm, alpha):
    t, d = x.shape
    row = lambda i: (i, 0)
    fixed = lambda i: (0, 0)
    return pl.pallas_call(
        functools.partial(_outproj_kernel, alpha=alpha),
        grid=(t // tm,),
        in_specs=[
            pl.BlockSpec((tm, d), row),
            pl.BlockSpec((tm, y_conv.shape[1]), row),
            pl.BlockSpec((tm, y_gla.shape[1]), row),
            pl.BlockSpec(w_c.shape, fixed),
            pl.BlockSpec(w_g.shape, fixed),
            pl.BlockSpec((1, d), fixed),
            pl.BlockSpec((1, d), fixed),
        ],
        out_specs=pl.BlockSpec((tm, d), row),
        out_shape=jax.ShapeDtypeStruct((t, d), F32),
        compiler_params=_cparams(1),
        name="outproj_ln",
    )(x, y_conv, y_gla, w_c, w_g, ln_g, ln_b)


def _extract_topk(s):
    n = s.shape[0]
    rows = lax.broadcasted_iota(jnp.int32, s.shape, 0).astype(F32)
    rank = jnp.full(s.shape, NOT_LISTED, F32)
    vals = []
    for r in range(PEER_TOPK):
        m = jnp.max(s, axis=0, keepdims=True)
        first = jnp.min(jnp.where(s == m, rows, float(n)), axis=0, keepdims=True)
        hit = rows == first
        rank = jnp.where(hit, float(r), rank)
        s = jnp.where(hit, -jnp.inf, s)
        vals.append(m)
    return vals, rank


def _sort_pairs(lo, hi):
    def merge(lo, hi, r):
        step = r * 2
        if step < hi - lo:
            yield from merge(lo, hi, step)
            yield from merge(lo + r, hi, step)
            yield from [(i, i + r) for i in range(lo + r, hi - r, step)]
        else:
            yield (lo, lo + r)

    if hi - lo >= 1:
        mid = lo + (hi - lo) // 2
        yield from _sort_pairs(lo, mid)
        yield from _sort_pairs(mid + 1, hi)
        yield from merge(lo, hi, 1)


def _topk_values(s):
    k = PEER_TOPK
    v = [s[r:r + SUBLANES, :] for r in range(0, s.shape[0], SUBLANES)]
    assert len(v) >= k and len(v) & (len(v) - 1) == 0
    for i, j in _sort_pairs(0, len(v) - 1):
        v[i], v[j] = jnp.maximum(v[i], v[j]), jnp.minimum(v[i], v[j])
    vals = []
    taken = jnp.zeros(v[0].shape, F32)
    for t in range(k):
        m = jnp.max(v[0], axis=0, keepdims=True)
        hit = v[0] == m
        taken = taken + jnp.where(hit, 1.0, 0.0)
        for r in range(k - 1 - t):
            v[r] = jnp.where(hit, v[r + 1], v[r])
        vals.append(m)
    equal = jnp.sum(taken, axis=0, keepdims=True) != float(k)
    for t in range(k - 1):
        equal = equal | (vals[t] == vals[t + 1])
    at_least = jnp.sum(jnp.where(s >= vals[k - 1], 1.0, 0.0), axis=0, keepdims=True)
    return vals, equal | (at_least != float(k))


def _dup16(x):
    bits = lax.bitcast_convert_type(x.astype(BF16).astype(F32), jnp.uint32)
    return lax.bitcast_convert_type(bits | (bits >> 16), jnp.int32)


def _route_head(s1, s2, exact):
    k = PEER_TOPK
    assert k == 16
    if exact:
        vals1, rank1 = _extract_topk(s1)
        vals2, rank2 = _extract_topk(s2)
        unusable = None
    else:
        vals1, equal1 = _topk_values(s1)
        vals2, equal2 = _topk_values(s2)
        unusable = equal1 | equal2
    v2 = jnp.concatenate(vals2, axis=0)
    v1_tail = jnp.concatenate(vals1[8:], axis=0)
    ev2 = jnp.exp(v2 - vals2[0])
    ev1 = [jnp.exp(vals1[a] - vals1[0]) for a in range(8)]
    ev1_tail = jnp.exp(v1_tail - vals1[0])
    cand = jnp.concatenate([vals1[0] + v2] + [vals1[a] + v2[0:8] for a in range(1, 8)]
                           + [v1_tail + vals2[0]], axis=0)
    gate = jnp.concatenate([ev1[0] * ev2] + [ev1[a] * ev2[0:8] for a in range(1, 8)]
                           + [ev1_tail * ev2[0:1]], axis=0)
    r = lax.broadcasted_iota(jnp.int32, cand.shape, 0)
    a_mid = 1 + ((r - 16) >> 3)
    b_mid = (r - 16) & 7
    head_rows = r < 16
    tail_rows = r >= 72
    flat = jnp.where(head_rows, r, jnp.where(tail_rows, (r - 64) * k, a_mid * k + b_mid)).astype(F32)
    valid = head_rows | tail_rows | ((a_mid + 1) * (b_mid + 1) <= k)
    cand = jnp.where(valid, cand, -jnp.inf)
    sel = jnp.zeros(cand.shape, F32)
    for _ in range(k):
        m = jnp.max(cand, axis=0, keepdims=True)
        if exact:
            first = jnp.min(jnp.where(cand == m, flat, float(k * k)), axis=0, keepdims=True)
            hit = flat == first
        else:
            hit = cand == m
        sel = jnp.where(hit, 1.0, sel)
        cand = jnp.where(hit, -jnp.inf, cand)
    z = jnp.sum(sel * gate, axis=0, keepdims=True)
    heights = [jnp.sum(sel[0:16], axis=0, keepdims=True)]
    heights += [jnp.sum(sel[8 + 8 * a:16 + 8 * a], axis=0, keepdims=True) for a in range(1, 8)]
    heights += [sel[64 + a:65 + a] for a in range(8, 16)]
    na = jnp.full(s1.shape, NOT_LISTED, F32)
    bm = jnp.full(s2.shape, -2.0 * NOT_LISTED, F32)
    for a in range(k):
        na = jnp.where((rank1 == float(a)) if exact else (s1 == vals1[a]), 1.0 - heights[a], na)
        bm = jnp.where((rank2 == float(a)) if exact else (s2 == vals2[a]), -float(a), bm)
    e1 = jnp.exp(s1 - vals1[0]) / z
    e2 = jnp.exp(s2 - vals2[0])
    if not exact:
        unusable = unusable | (jnp.sum(sel, axis=0, keepdims=True) != float(k))
    return na, bm, e1, e2, unusable


def _route_kernel(x_ref, wq_ref, keys_ref, na_ref, bm_ref, e1_ref, e2_ref, s_ref):
    xq = jnp.dot(x_ref[...].astype(BF16), wq_ref[...], preferred_element_type=F32)
    dsub = keys_ref.shape[-1]
    for h in range(PEER_HEADS):
        for p in range(2):
            off = (h * 2 + p) * dsub
            qh, qm, ql = _split3(xq[:, off:off + dsub])
            kh, km, kl = _split3(keys_ref[h * 2 + p])
            s_ref[p] = (_dot_nt(kh, qh) + (_dot_nt(kh, qm) + _dot_nt(km, qh))
                        + (_dot_nt(kh, ql) + _dot_nt(km, qm) + _dot_nt(kl, qh)))

        def emit(exact, h=h):
            na, bm, e1, e2, unusable = _route_head(s_ref[0], s_ref[1], exact)
            na_ref[h] = _dup16(na)
            bm_ref[h] = bm.astype(BF16)
            e1_ref[h] = _dup16(e1)
            e2_ref[h] = e2.astype(BF16)
            return unusable

        unusable = emit(False)

        @pl.when(jnp.sum(jnp.where(unusable, 1.0, 0.0)) > 0.0)
        def _():
            emit(True)


def _route(x1, wq, keys, *, tm):
    t, d = x1.shape
    nk = keys.shape[1]
    ospec = pl.BlockSpec((PEER_HEADS, nk, tm), lambda i: (0, 0, i))
    words = jax.ShapeDtypeStruct((PEER_HEADS, nk, t), jnp.int32)
    halfs = jax.ShapeDtypeStruct((PEER_HEADS, nk, t), BF16)
    return pl.pallas_call(
        _route_kernel,
        grid=(t // tm,),
        in_specs=[
            pl.BlockSpec((tm, d), lambda i: (i, 0)),
            pl.BlockSpec(wq.shape, lambda i: (0, 0)),
            pl.BlockSpec(keys.shape, lambda i: (0, 0, 0)),
        ],
        out_specs=[ospec] * 4,
        out_shape=(words, halfs, words, halfs),
        scratch_shapes=[pltpu.VMEM((2, nk, tm), F32)],
        compiler_params=_cparams(1),
        name="peer_route",
    )(x1, wq, keys)


def _gelu(x):
    return 0.5 * x * (1.0 + lax.erf(x * 0.7071067811865476))


def _peer_kernel(x_ref, na_ref, bm_ref, e1_ref, e2_ref, u_ref, vt_ref, g_ref, b_ref, y_ref,
                 xt_ref, h0_ref, h1_ref, p0_ref, p1_ref, acc_ref, *, ib, nblk, alpha):
    step = pl.program_id(1)
    nk = bm_ref.shape[1]
    tn = xt_ref.shape[1]

    def stage(h_new, h_old, p_new, p_old, project, weigh, combine):
        parts = 4
        group = 16
        hm = ib * nk // parts
        dm = acc_ref.shape[0] // parts
        for q in range(parts):
            if project:
                h_new[q * hm:(q + 1) * hm, :] = jnp.dot(_unpack_rows(u_ref, q * hm, hm), xt_ref[...],
                                                        preferred_element_type=F32)
            if weigh:
                for ii in range(q * ib // parts, (q + 1) * ib // parts):
                    row = lambda ref, h: pltpu.bitcast(
                        jnp.broadcast_to(ref[h, ii:ii + 1, :], (group // 2, tn)), BF16)
                    na_b = [row(na_ref, h) for h in range(PEER_HEADS)]
                    e1_b = [row(e1_ref, h) for h in range(PEER_HEADS)]
                    for r0 in range(0, nk, group):
                        cols = slice(r0, r0 + group)
                        w = jnp.zeros((group, tn), BF16)
                        for h in range(PEER_HEADS):
                            w = w + jnp.where(bm_ref[h, cols, :] >= na_b[h], e2_ref[h, cols, :] * e1_b[h],
                                              jnp.zeros((group, tn), BF16))
                        rows = slice(ii * nk + r0, ii * nk + r0 + group)
                        p_new[rows, :] = _gelu(h_old[rows, :]).astype(BF16) * w
            if combine:
                acc_ref[q * dm:(q + 1) * dm, :] += jnp.dot(_unpack_rows(vt_ref, q * dm, dm), p_old[...],
                                                           preferred_element_type=F32)

    @pl.when(step == 0)
    def _():
        for c0 in range(0, tn, LANES):
            xt_ref[:, c0:c0 + LANES] = x_ref[c0:c0 + LANES, :].T.astype(BF16)
        acc_ref[...] = jnp.zeros_like(acc_ref)
        stage(h0_ref, None, None, None, True, False, False)

    @pl.when(step == 1)
    def _():
        stage(h1_ref, h0_ref, p1_ref, None, True, True, False)

    mid = (step >= 2) & (step < nblk)

    @pl.when(mid & (step % 2 == 0))
    def _():
        stage(h0_ref, h1_ref, p0_ref, p1_ref, True, True, True)

    @pl.when(mid & (step % 2 == 1))
    def _():
        stage(h1_ref, h0_ref, p1_ref, p0_ref, True, True, True)

    @pl.when(step == nblk)
    def _():
        stage(None, h1_ref, p0_ref, p1_ref, False, True, True)

    @pl.when(step == nblk + 1)
    def _():
        stage(None, None, None, p0_ref, False, False, True)
        for c0 in range(0, tn, LANES):
            rows = slice(c0, c0 + LANES)
            y_ref[rows, :] = _layer_norm(alpha * x_ref[rows, :] + acc_ref[:, rows].T, g_ref[...], b_ref[...])


def _peer_dense(x1, na, bm, e1, e2, u_words, vt_words, ln_g, ln_b, *, tn, ib, alpha):
    t, d = x1.shape
    nk = bm.shape[1]
    nblk = 2 * u_words.shape[0] // (ib * nk)
    assert nblk % 2 == 0 and nblk >= 4
    last = nblk - 1
    weigh_blk = lambda s: jnp.clip(s - 1, 0, last)
    row_code = pl.BlockSpec((PEER_HEADS, ib, tn), lambda i, s: (0, weigh_blk(s), i))
    col_code = pl.BlockSpec((PEER_HEADS, nk, tn), lambda i, s: (0, 0, i))
    return pl.pallas_call(
        functools.partial(_peer_kernel, ib=ib, nblk=nblk, alpha=alpha),
        grid=(t // tn, nblk + 2),
        in_specs=[
            pl.BlockSpec((tn, d), lambda i, s: (i, 0)),
            row_code, col_code, row_code, col_code,
            pl.BlockSpec((ib * nk // 2, d), lambda i, s: (jnp.minimum(s, last), 0)),
            pl.BlockSpec((d // 2, ib * nk), lambda i, s: (0, jnp.clip(s - 2, 0, last))),
            pl.BlockSpec((1, d), lambda i, s: (0, 0)),
            pl.BlockSpec((1, d), lambda i, s: (0, 0)),
        ],
        out_specs=pl.BlockSpec((tn, d), lambda i, s: (i, 0)),
        out_shape=jax.ShapeDtypeStruct((t, d), F32),
        scratch_shapes=[pltpu.VMEM((d, tn), BF16),
                        pltpu.VMEM((ib * nk, tn), F32), pltpu.VMEM((ib * nk, tn), F32),
                        pltpu.VMEM((ib * nk, tn), BF16), pltpu.VMEM((ib * nk, tn), BF16),
                        pltpu.VMEM((d, tn), F32)],
        compiler_params=_cparams(2),
        name="peer_dense",
    )(x1, na, bm, e1, e2, u_words, vt_words, ln_g, ln_b)


def _pick(n, pref):
    return pref if n % pref == 0 else n


def _trunk_layer(x, conv_buf, gla_state, prm, alpha):
    b, l, d = x.shape
    t = b * l
    xf = x.reshape(t, d)
    tm = _pick(t, 256)

    u, qk, v, g, la = _inproj(xf, prm["w_main"], prm["w_f"], prm["gw_pad"], prm["gate_b"],
                              tm=_pick(t, 512))
    conv_dim = u.shape[-1]

    buf_pad = jnp.pad(conv_buf, ((0, 0), (HALO - (CONV_WIDTH - 1), 0), (0, 0)))
    y_conv, conv_new = _conv_mixer(u.reshape(b, l, conv_dim), buf_pad, prm["conv_w"], prm["conv_b"],
                                   prm["conv_ln_g"], prm["conv_ln_b"], tl=_pick(l, 256))

    c = _pick(l, GLA_CHUNK)
    s0_t = jnp.swapaxes(gla_state, -1, -2)
    y_gla, s_t = _gla_mixer(qk.reshape(b, l, -1), v.reshape(b, l, -1), g.reshape(b, l, -1),
                            la.reshape(b, l, -1), s0_t, prm["gla_norm_g"], c=c)
    gla_new = jnp.swapaxes(s_t, -1, -2)

    x1 = _outproj(xf, y_conv.reshape(t, -1), y_gla.reshape(t, -1), prm["w_out_c"], prm["w_out_g"],
                  prm["ln1_g"], prm["ln1_b"], tm=tm, alpha=alpha)

    na, bm, e1, e2 = _route(x1, prm["wq"], prm["keys"], tm=tm)
    y = _peer_dense(x1, na, bm, e1, e2, prm["u_words"], prm["vt_words"], prm["ln2_g"], prm["ln2_b"],
                    tn=_pick(t, 512), ib=8, alpha=alpha)
    return y.reshape(b, l, d), conv_new, gla_new


def _prep_layer(i, w_in, conv_w, conv_b, conv_ln_g, conv_ln_b, gla_gate_w, gla_gate_b, gla_norm_g,
                w_out, ln1_g, ln1_b, peer_w_query, peer_sub_keys, peer_u, peer_v, ln2_g, ln2_b):
    conv_dim = conv_w.shape[-1]
    half = conv_dim // 2
    wi = w_in[i]
    a_w, g_w, rest = wi[:, :conv_dim], wi[:, conv_dim:2 * conv_dim], wi[:, 2 * conv_dim:]
    n_rest = rest.shape[1] - GLA_RANK
    w_main = jnp.concatenate([a_w[:, :half], g_w[:, :half], a_w[:, half:], g_w[:, half:],
                              rest[:, :n_rest]], axis=1).astype(BF16)
    w_f = jnp.pad(rest[:, n_rest:], ((0, 0), (0, LANES - GLA_RANK))).astype(BF16)
    gw_pad = jnp.pad(gla_gate_w[i], ((0, LANES - GLA_RANK), (0, 0)))
    row = lambda a: a[i][None, :]
    keys = peer_sub_keys[i]
    return dict(
        w_main=w_main, w_f=w_f, gw_pad=gw_pad, gate_b=row(gla_gate_b),
        conv_w=conv_w[i], conv_b=row(conv_b), conv_ln_g=row(conv_ln_g), conv_ln_b=row(conv_ln_b),
        gla_norm_g=row(gla_norm_g),
        w_out_c=w_out[i][:conv_dim].astype(BF16), w_out_g=w_out[i][conv_dim:].astype(BF16),
        ln1_g=row(ln1_g), ln1_b=row(ln1_b),
        wq=peer_w_query[i].astype(BF16),
        keys=keys.reshape(keys.shape[0] * 2, keys.shape[2], keys.shape[3]),
        u_words=_pack_table(peer_u[i], transpose=False), vt_words=_pack_table(peer_v[i], transpose=True),
        ln2_g=row(ln2_g), ln2_b=row(ln2_b),
    )


def kernel(x_prompt, x_sample, state_conv, state_gla, w_in, conv_w, conv_b, conv_ln_g, conv_ln_b,
           gla_gate_w, gla_gate_b, gla_norm_g, w_out, ln1_g, ln1_b,
           peer_w_query, peer_sub_keys, peer_u, peer_v, ln2_g, ln2_b):
    depth = w_in.shape[0]
    alpha = (2.0 * depth) ** 0.25
    yp, ys = x_prompt, x_sample
    conv_p, gla_p, conv_s, gla_s = [], [], [], []
    for i in range(depth):
        prm = _prep_layer(i, w_in, conv_w, conv_b, conv_ln_g, conv_ln_b, gla_gate_w, gla_gate_b,
                          gla_norm_g, w_out, ln1_g, ln1_b, peer_w_query, peer_sub_keys,
                          peer_u, peer_v, ln2_g, ln2_b)
        bp = x_prompt.shape[0]
        zero_conv = jnp.zeros((bp,) + state_conv.shape[2:], x_prompt.dtype)
        zero_gla = jnp.zeros((bp,) + state_gla.shape[2:], state_gla.dtype)
        yp, cp, gp = _trunk_layer(yp, zero_conv, zero_gla, prm, alpha)
        ys, cs, gs = _trunk_layer(ys, state_conv[i], state_gla[i], prm, alpha)
        conv_p.append(cp)
        gla_p.append(gp)
        conv_s.append(cs)
        gla_s.append(gs)
    return (yp, ys, jnp.stack(conv_p), jnp.stack(gla_p), jnp.stack(conv_s), jnp.stack(gla_s))
```

```python
import functools

import jax
import jax.numpy as jnp
from jax import lax
from jax.experimental import pallas as pl
from jax.experimental.pallas import tpu as pltpu

F32 = jnp.float32
BF16 = jnp.bfloat16

CONV_WIDTH = 31
GLA_HEADS = 4
GLA_RANK = 16
GLA_TAU = 16.0
PEER_HEADS = 8
PEER_NKEYS = 128
PEER_TOPK = 16
LN_EPS = 1e-5

ROUTE_GROUP = 4
GLA_CHUNK = 64
GLA_SUB = 16
LANES = 128
SUBLANES = 8
HALO = 32
NOT_LISTED = 8192.0
VMEM_LIMIT = 56 * 1024 * 1024


def _cparams(n_axes):
    return pltpu.CompilerParams(dimension_semantics=("arbitrary",) * n_axes,
                                vmem_limit_bytes=VMEM_LIMIT)


def _layer_norm(x, g, b):
    mu = jnp.mean(x, axis=-1, keepdims=True)
    xc = x - mu
    var = jnp.mean(xc * xc, axis=-1, keepdims=True)
    return xc * lax.rsqrt(var + LN_EPS) * g + b


def _sigmoid(x):
    return 1.0 / (1.0 + jnp.exp(-x))


def _log_sigmoid(x):
    return jnp.minimum(x, 0.0) - jnp.log(1.0 + jnp.exp(-jnp.abs(x)))


def _dot_nt(a, b):
    return lax.dot_general(a, b, (((1,), (1,)), ((), ())), preferred_element_type=F32)


def _dot_tn(a, b):
    return lax.dot_general(a, b, (((0,), (0,)), ((), ())), preferred_element_type=F32)


def _pack_kernel(w_ref, o_ref, *, transpose):
    w = w_ref[...]
    o_ref[...] = pltpu.bitcast((w.T if transpose else w).astype(BF16), jnp.uint32)


def _pack_table(w, *, transpose, rows=512):
    r, c = w.shape
    if transpose:
        out_shape, out_block, out_index = (c // 2, r), (c // 2, rows), lambda i: (0, i)
    else:
        out_shape, out_block, out_index = (r // 2, c), (rows // 2, c), lambda i: (i, 0)
    return pl.pallas_call(
        functools.partial(_pack_kernel, transpose=transpose),
        grid=(r // rows,),
        in_specs=[pl.BlockSpec((rows, c), lambda i: (i, 0))],
        out_specs=pl.BlockSpec(out_block, out_index),
        out_shape=jax.ShapeDtypeStruct(out_shape, jnp.uint32),
        compiler_params=_cparams(1),
        name="pack_table",
    )(w)


def _unpack_rows(words_ref, row0, nrows):
    return pltpu.bitcast(words_ref[row0 // 2:(row0 + nrows) // 2, :], BF16)


def _split3(x):
    hi = x.astype(BF16)
    r = x - hi.astype(F32)
    mid = r.astype(BF16)
    lo = (r - mid.astype(F32)).astype(BF16)
    return hi, mid, lo


def _inproj_kernel(x_ref, w_ref, wf_ref, gw_ref, gb_ref,
                   u_ref, qk_ref, v_ref, g_ref, la_ref, xb_ref, *, half):
    xb_ref[...] = x_ref[...].astype(BF16)
    blk = 2 * half
    proj = lambda j: jnp.dot(xb_ref[...], w_ref[:, j * blk:(j + 1) * blk], preferred_element_type=F32)
    for j in range(2):
        z = proj(j)
        u_ref[:, j * half:(j + 1) * half] = z[:, :half] * _sigmoid(z[:, half:])
    qk_ref[...] = proj(2).astype(qk_ref.dtype)
    v_ref[...] = proj(3).astype(v_ref.dtype)
    g_ref[...] = proj(4).astype(g_ref.dtype)
    f = jnp.dot(xb_ref[...], wf_ref[...], preferred_element_type=F32)
    pre = jnp.dot(f, gw_ref[...], preferred_element_type=F32,
                  precision=lax.Precision.HIGHEST) + gb_ref[...]
    la_ref[...] = _log_sigmoid(pre) * (1.0 / GLA_TAU)


def _inproj(x, w_main, w_f, gw_pad, gb, *, tm):
    t, d = x.shape
    conv_dim = 1024
    qk2 = 1024
    v_dim = 1024
    assert w_main.shape[1] == 2 * conv_dim + qk2 + 2 * v_dim
    resident = dict(pipeline_mode=pl.Buffered(1))
    out_shapes = (
        jax.ShapeDtypeStruct((t, conv_dim), F32),
        jax.ShapeDtypeStruct((t, qk2), BF16),
        jax.ShapeDtypeStruct((t, v_dim), BF16),
        jax.ShapeDtypeStruct((t, v_dim), BF16),
        jax.ShapeDtypeStruct((t, qk2 // 2), F32),
    )
    row = lambda i: (i, 0)
    fixed = lambda i: (0, 0)
    return pl.pallas_call(
        functools.partial(_inproj_kernel, half=conv_dim // 2),
        grid=(t // tm,),
        in_specs=[
            pl.BlockSpec((tm, d), row),
            pl.BlockSpec(w_main.shape, fixed, **resident),
            pl.BlockSpec((d, LANES), fixed, **resident),
            pl.BlockSpec((LANES, qk2 // 2), fixed, **resident),
            pl.BlockSpec((1, qk2 // 2), fixed, **resident),
        ],
        out_specs=[
            pl.BlockSpec((tm, conv_dim), row),
            pl.BlockSpec((tm, qk2), row),
            pl.BlockSpec((tm, v_dim), row),
            pl.BlockSpec((tm, v_dim), row),
            pl.BlockSpec((tm, qk2 // 2), row),
        ],
        out_shape=out_shapes,
        scratch_shapes=[pltpu.VMEM((tm, d), BF16)],
        compiler_params=_cparams(1),
        name="inproj",
    )(x, w_main, w_f, gw_pad, gb)


def _conv_kernel(buf_ref, u_ref, w_ref, cb_ref, lg_ref, lb_ref, y_ref, st_ref, ext_ref, sh_ref, wb_ref,
                 *, tl, rs):
    l = pl.program_id(1)

    @pl.when(l == 0)
    def _():
        ext_ref[0:HALO, :] = buf_ref[0]

    @pl.when(l > 0)
    def _():
        ext_ref[0:HALO, :] = ext_ref[tl:tl + HALO, :]

    ext_ref[HALO:HALO + tl, :] = u_ref[0]
    first = HALO - (CONV_WIDTH - 1)
    span = sh_ref.shape[1]
    for r in range(1, SUBLANES):
        sh_ref[r - 1] = ext_ref[r:r + span, :]

    def window(row, n):
        shift = row % SUBLANES
        base = row - shift
        return ext_ref[base:base + n, :] if shift == 0 else sh_ref[shift - 1, base:base + n, :]

    @pl.when((pl.program_id(0) == 0) & (l == 0))
    def _():
        for j in range(CONV_WIDTH):
            wb_ref[j] = jnp.broadcast_to(w_ref[j:j + 1, :], wb_ref.shape[1:])

    for r0 in range(0, tl, rs):
        acc = window(first + r0, rs) * wb_ref[0]
        for j in range(1, CONV_WIDTH):
            acc = acc + window(first + r0 + j, rs) * wb_ref[j]
        y = _layer_norm(acc + cb_ref[...], lg_ref[...], lb_ref[...])
        y_ref[0, r0:r0 + rs, :] = (y * _sigmoid(y)).astype(y_ref.dtype)
    st_ref[0] = ext_ref[tl + first:tl + HALO, :]


def _conv_mixer(u, buf_pad, conv_w, conv_b, ln_g, ln_b, *, tl):
    b, l, c = u.shape
    rs = min(16, tl)
    fixed = lambda i, j: (0, 0)
    return pl.pallas_call(
        functools.partial(_conv_kernel, tl=tl, rs=rs),
        grid=(b, l // tl),
        in_specs=[
            pl.BlockSpec((1, HALO, c), lambda i, j: (i, 0, 0)),
            pl.BlockSpec((1, tl, c), lambda i, j: (i, j, 0)),
            pl.BlockSpec((CONV_WIDTH, c), fixed),
            pl.BlockSpec((1, c), fixed),
            pl.BlockSpec((1, c), fixed),
            pl.BlockSpec((1, c), fixed),
        ],
        out_specs=[
            pl.BlockSpec((1, tl, c), lambda i, j: (i, j, 0)),
            pl.BlockSpec((1, CONV_WIDTH - 1, c), lambda i, j: (i, 0, 0)),
        ],
        out_shape=(jax.ShapeDtypeStruct((b, l, c), BF16),
                   jax.ShapeDtypeStruct((b, CONV_WIDTH - 1, c), F32)),
        scratch_shapes=[pltpu.VMEM((HALO + tl, c), F32),
                        pltpu.VMEM((SUBLANES - 1, tl + HALO - SUBLANES, c), F32),
                        pltpu.VMEM((CONV_WIDTH, rs, c), F32)],
        compiler_params=_cparams(2),
        name="conv_mixer",
    )(buf_pad, u, conv_w, conv_b, ln_g, ln_b)


def _gla_kernel(qk_ref, v_ref, g_ref, la_ref, s0_ref, ng_ref, y_ref, sout_ref, st_ref, *, c, dk, dv):
    l = pl.program_id(1)

    @pl.when(l == 0)
    def _():
        st_ref[...] = s0_ref[0]

    hk = GLA_HEADS * dk
    la = la_ref[0]
    rows = lax.broadcasted_iota(jnp.int32, (c, c), 0)
    cols = lax.broadcasted_iota(jnp.int32, (c, c), 1)
    causal = rows >= cols
    tril = jnp.where(causal, 1.0, 0.0).astype(BF16)
    hi, mid, lo = _split3(la)
    bcum = (jnp.dot(tril, hi, preferred_element_type=F32)
            + jnp.dot(tril, mid, preferred_element_type=F32)
            + jnp.dot(tril, lo, preferred_element_type=F32))
    b_last = bcum[c - 1:c, :]
    q = qk_ref[0, :, :hk].astype(F32) * (dk ** -0.5)
    k = qk_ref[0, :, hk:].astype(F32)
    q_in = (q * jnp.exp(bcum)).astype(BF16)
    k_out = (k * jnp.exp(b_last - bcum)).astype(BF16)
    row_id = lax.broadcasted_iota(jnp.int32, (c, dk), 0)

    for h in range(GLA_HEADS):
        ks = slice(h * dk, (h + 1) * dk)
        vs = slice(h * dv, (h + 1) * dv)
        bh = bcum[:, ks]
        qh = q[:, ks]
        kh = k[:, ks]
        vh = v_ref[0, :, vs].astype(BF16)
        st = st_ref[h]
        o = _dot_nt(q_in[:, ks], st.astype(BF16))
        blocks = []
        for i in range(c // GLA_SUB):
            r0 = i * GLA_SUB
            base = bh[r0 - 1:r0, :] if i > 0 else jnp.zeros((1, dk), F32)
            q_t = qh[r0:r0 + GLA_SUB, :] * jnp.exp(bh[r0:r0 + GLA_SUB, :] - base)
            k_t = kh * jnp.exp(jnp.where(row_id < r0 + GLA_SUB, base - bh, -1e30))
            blocks.append(_dot_nt(q_t.astype(BF16), k_t.astype(BF16)))
        scores = jnp.where(causal, jnp.concatenate(blocks, axis=0), 0.0)
        o = o + jnp.dot(scores.astype(BF16), vh, preferred_element_type=F32)
        st_ref[h] = st * jnp.exp(b_last[:, ks]) + _dot_tn(vh, k_out[:, ks])
        o = o * lax.rsqrt(jnp.mean(o * o, axis=-1, keepdims=True) + LN_EPS)
        gh = g_ref[0, :, vs].astype(F32)
        y_ref[0, :, vs] = (o * ng_ref[:, vs] * (gh * _sigmoid(gh))).astype(y_ref.dtype)

    sout_ref[0] = st_ref[...]


def _gla_mixer(qk, v, g, la, s0_t, norm_g, *, c):
    b, l, hk2 = qk.shape
    dk = hk2 // (2 * GLA_HEADS)
    dv = v.shape[-1] // GLA_HEADS
    tile = lambda i, j: (i, j, 0)
    return pl.pallas_call(
        functools.partial(_gla_kernel, c=c, dk=dk, dv=dv),
        grid=(b, l // c),
        in_specs=[
            pl.BlockSpec((1, c, hk2), tile),
            pl.BlockSpec((1, c, GLA_HEADS * dv), tile),
            pl.BlockSpec((1, c, GLA_HEADS * dv), tile),
            pl.BlockSpec((1, c, hk2 // 2), tile),
            pl.BlockSpec((1, GLA_HEADS, dv, dk), lambda i, j: (i, 0, 0, 0)),
            pl.BlockSpec((1, GLA_HEADS * dv), lambda i, j: (0, 0)),
        ],
        out_specs=[
            pl.BlockSpec((1, c, GLA_HEADS * dv), tile),
            pl.BlockSpec((1, GLA_HEADS, dv, dk), lambda i, j: (i, 0, 0, 0)),
        ],
        out_shape=(jax.ShapeDtypeStruct((b, l, GLA_HEADS * dv), BF16),
                   jax.ShapeDtypeStruct((b, GLA_HEADS, dv, dk), F32)),
        scratch_shapes=[pltpu.VMEM((GLA_HEADS, dv, dk), F32)],
        compiler_params=_cparams(2),
        name="gla_mixer",
    )(qk, v, g, la, s0_t, norm_g)


def _outproj_kernel(x_ref, yc_ref, yg_ref, wc_ref, wg_ref, g_ref, b_ref, o_ref, *, alpha):
    mix = (jnp.dot(yc_ref[...].astype(BF16), wc_ref[...], preferred_element_type=F32)
           + jnp.dot(yg_ref[...].astype(BF16), wg_ref[...], preferred_element_type=F32))
    o_ref[...] = _layer_norm(alpha * x_ref[...] + mix, g_ref[...], b_ref[...])


def _outproj(x, y_conv, y_gla, w_c, w_g, ln_g, ln_b, *, tm, alpha):
    t, d = x.shape
    row = lambda i: (i, 0)
    fixed = lambda i: (0, 0)
    return pl.pallas_call(
        functools.partial(_outproj_kernel, alpha=alpha),
        grid=(t // tm,),
        in_specs=[
            pl.BlockSpec((tm, d), row),
            pl.BlockSpec((tm, y_conv.shape[1]), row),
            pl.BlockSpec((tm, y_gla.shape[1]), row),
            pl.BlockSpec(w_c.shape, fixed),
            pl.BlockSpec(w_g.shape, fixed),
            pl.BlockSpec((1, d), fixed),
            pl.BlockSpec((1, d), fixed),
        ],
        out_specs=pl.BlockSpec((tm, d), row),
        out_shape=jax.ShapeDtypeStruct((t, d), F32),
        compiler_params=_cparams(1),
        name="outproj_ln",
    )(x, y_conv, y_gla, w_c, w_g, ln_g, ln_b)


def _extract_topk(s):
    n = s.shape[0]
    rows = lax.broadcasted_iota(jnp.int32, s.shape, 0).astype(F32)
    rank = jnp.full(s.shape, NOT_LISTED, F32)
    vals = []
    for r in range(PEER_TOPK):
        m = jnp.max(s, axis=0, keepdims=True)
        first = jnp.min(jnp.where(s == m, rows, float(n)), axis=0, keepdims=True)
        hit = rows == first
        rank = jnp.where(hit, float(r), rank)
        s = jnp.where(hit, -jnp.inf, s)
        vals.append(m)
    return vals, rank


def _sort_pairs(lo, hi):
    def merge(lo, hi, r):
        step = r * 2
        if step < hi - lo:
            yield from merge(lo, hi, step)
            yield from merge(lo + r, hi, step)
            yield from [(i, i + r) for i in range(lo + r, hi - r, step)]
        else:
            yield (lo, lo + r)

    if hi - lo >= 1:
        mid = lo + (hi - lo) // 2
        yield from _sort_pairs(lo, mid)
        yield from _sort_pairs(mid + 1, hi)
        yield from merge(lo, hi, 1)


def _topk_values(s):
    k = PEER_TOPK
    v = [s[r:r + SUBLANES, :] for r in range(0, s.shape[0], SUBLANES)]
    assert len(v) >= k and len(v) & (len(v) - 1) == 0
    for i, j in _sort_pairs(0, len(v) - 1):
        v[i], v[j] = jnp.maximum(v[i], v[j]), jnp.minimum(v[i], v[j])
    vals = []
    taken = jnp.zeros(v[0].shape, F32)
    for t in range(k):
        m = jnp.max(v[0], axis=0, keepdims=True)
        hit = v[0] == m
        taken = taken + jnp.where(hit, 1.0, 0.0)
        for r in range(k - 1 - t):
            v[r] = jnp.where(hit, v[r + 1], v[r])
        vals.append(m)
    equal = jnp.sum(taken, axis=0, keepdims=True) != float(k)
    for t in range(k - 1):
        equal = equal | (vals[t] == vals[t + 1])
    at_least = jnp.sum(jnp.where(s >= vals[k - 1], 1.0, 0.0), axis=0, keepdims=True)
    return vals, equal | (at_least != float(k))


def _dup16(x):
    bits = lax.bitcast_convert_type(x.astype(BF16).astype(F32), jnp.uint32)
    return lax.bitcast_convert_type(bits | (bits >> 16), jnp.int32)


def _route_head(s1, s2, exact):
    k = PEER_TOPK
    assert k == 16
    if exact:
        vals1, rank1 = _extract_topk(s1)
        vals2, rank2 = _extract_topk(s2)
        unusable = None
    else:
        vals1, equal1 = _topk_values(s1)
        vals2, equal2 = _topk_values(s2)
        unusable = equal1 | equal2
    v2 = jnp.concatenate(vals2, axis=0)
    v1_tail = jnp.concatenate(vals1[8:], axis=0)
    ev2 = jnp.exp(v2 - vals2[0])
    ev1 = [jnp.exp(vals1[a] - vals1[0]) for a in range(8)]
    ev1_tail = jnp.exp(v1_tail - vals1[0])
    cand = jnp.concatenate([vals1[0] + v2] + [vals1[a] + v2[0:8] for a in range(1, 8)]
                           + [v1_tail + vals2[0]], axis=0)
    gate = jnp.concatenate([ev1[0] * ev2] + [ev1[a] * ev2[0:8] for a in range(1, 8)]
                           + [ev1_tail * ev2[0:1]], axis=0)
    r = lax.broadcasted_iota(jnp.int32, cand.shape, 0)
    a_mid = 1 + ((r - 16) >> 3)
    b_mid = (r - 16) & 7
    head_rows = r < 16
    tail_rows = r >= 72
    flat = jnp.where(head_rows, r, jnp.where(tail_rows, (r - 64) * k, a_mid * k + b_mid)).astype(F32)
    valid = head_rows | tail_rows | ((a_mid + 1) * (b_mid + 1) <= k)
    cand = jnp.where(valid, cand, -jnp.inf)
    sel = jnp.zeros(cand.shape, F32)
    for _ in range(k):
        m = jnp.max(cand, axis=0, keepdims=True)
        if exact:
            first = jnp.min(jnp.where(cand == m, flat, float(k * k)), axis=0, keepdims=True)
            hit = flat == first
        else:
            hit = cand == m
        sel = jnp.where(hit, 1.0, sel)
        cand = jnp.where(hit, -jnp.inf, cand)
    z = jnp.sum(sel * gate, axis=0, keepdims=True)
    heights = [jnp.sum(sel[0:16], axis=0, keepdims=True)]
    heights += [jnp.sum(sel[8 + 8 * a:16 + 8 * a], axis=0, keepdims=True) for a in range(1, 8)]
    heights += [sel[64 + a:65 + a] for a in range(8, 16)]
    na = jnp.full(s1.shape, NOT_LISTED, F32)
    bm = jnp.full(s2.shape, -2.0 * NOT_LISTED, F32)
    for a in range(k):
        na = jnp.where((rank1 == float(a)) if exact else (s1 == vals1[a]), 1.0 - heights[a], na)
        bm = jnp.where((rank2 == float(a)) if exact else (s2 == vals2[a]), -float(a), bm)
    e1 = jnp.exp(s1 - vals1[0]) / z
    e2 = jnp.exp(s2 - vals2[0])
    if not exact:
        unusable = unusable | (jnp.sum(sel, axis=0, keepdims=True) != float(k))
    return na, bm, e1, e2, unusable


def _route_kernel(x_ref, wq_ref, keys_ref, na_ref, bm_ref, e1_ref, e2_ref, s_ref):
    xq = jnp.dot(x_ref[...].astype(BF16), wq_ref[...], preferred_element_type=F32)
    dsub = keys_ref.shape[-1]
    for sub in range(2 * PEER_HEADS):
        qh, qm, ql = _split3(xq[:, sub * dsub:(sub + 1) * dsub])
        kh, km, kl = _split3(keys_ref[sub])
        s_ref[sub] = (_dot_nt(kh, qh) + (_dot_nt(kh, qm) + _dot_nt(km, qh))
                      + (_dot_nt(kh, ql) + _dot_nt(km, qm) + _dot_nt(kl, qh)))

    def emit(h, exact):
        na, bm, e1, e2, unusable = _route_head(s_ref[2 * h], s_ref[2 * h + 1], exact)
        na_ref[h] = _dup16(na)
        bm_ref[h] = bm.astype(BF16)
        e1_ref[h] = _dup16(e1)
        e2_ref[h] = e2.astype(BF16)
        return unusable

    for h0 in range(0, PEER_HEADS, ROUTE_GROUP):
        group = range(h0, h0 + ROUTE_GROUP)
        unusable = functools.reduce(jnp.logical_or, [emit(h, False) for h in group])

        @pl.when(jnp.sum(jnp.where(unusable, 1.0, 0.0)) > 0.0)
        def _():
            for h in group:
                emit(h, True)


def _route(x1, wq, keys, *, tm):
    t, d = x1.shape
    nk = keys.shape[1]
    ospec = pl.BlockSpec((PEER_HEADS, nk, tm), lambda i: (0, 0, i))
    words = jax.ShapeDtypeStruct((PEER_HEADS, nk, t), jnp.int32)
    halfs = jax.ShapeDtypeStruct((PEER_HEADS, nk, t), BF16)
    return pl.pallas_call(
        _route_kernel,
        grid=(t // tm,),
        in_specs=[
            pl.BlockSpec((tm, d), lambda i: (i, 0)),
            pl.BlockSpec(wq.shape, lambda i: (0, 0)),
            pl.BlockSpec(keys.shape, lambda i: (0, 0, 0)),
        ],
        out_specs=[ospec] * 4,
        out_shape=(words, halfs, words, halfs),
        scratch_shapes=[pltpu.VMEM((2 * PEER_HEADS, nk, tm), F32)],
        compiler_params=_cparams(1),
        name="peer_route",
    )(x1, wq, keys)


def _gelu(x):
    return 0.5 * x * (1.0 + lax.erf(x * 0.7071067811865476))


def _peer_kernel(x_ref, na_ref, bm_ref, e1_ref, e2_ref, u_ref, vt_ref, g_ref, b_ref, y_ref,
                 xt_ref, h0_ref, h1_ref, p0_ref, p1_ref, acc_ref, *, ib, nblk, alpha):
    step = pl.program_id(1)
    nk = bm_ref.shape[1]
    tn = xt_ref.shape[1]

    def stage(h_new, h_old, p_new, p_old, project, weigh, combine):
        parts = 4
        group = 16
        hm = ib * nk // parts
        dm = acc_ref.shape[0] // parts
        for q in range(parts):
            if project:
                h_new[q * hm:(q + 1) * hm, :] = jnp.dot(_unpack_rows(u_ref, q * hm, hm), xt_ref[...],
                                                        preferred_element_type=F32)
            if weigh:
                for ii in range(q * ib // parts, (q + 1) * ib // parts):
                    row = lambda ref, h: pltpu.bitcast(
                        jnp.broadcast_to(ref[h, ii:ii + 1, :], (group // 2, tn)), BF16)
                    na_b = [row(na_ref, h) for h in range(PEER_HEADS)]
                    e1_b = [row(e1_ref, h) for h in range(PEER_HEADS)]
                    for r0 in range(0, nk, group):
                        cols = slice(r0, r0 + group)
                        w = jnp.zeros((group, tn), BF16)
                        for h in range(PEER_HEADS):
                            w = w + jnp.where(bm_ref[h, cols, :] >= na_b[h], e2_ref[h, cols, :] * e1_b[h],
                                              jnp.zeros((group, tn), BF16))
                        rows = slice(ii * nk + r0, ii * nk + r0 + group)
                        p_new[rows, :] = _gelu(h_old[rows, :]).astype(BF16) * w
            if combine:
                acc_ref[q * dm:(q + 1) * dm, :] += jnp.dot(_unpack_rows(vt_ref, q * dm, dm), p_old[...],
                                                           preferred_element_type=F32)

    @pl.when(step == 0)
    def _():
        for c0 in range(0, tn, LANES):
            xt_ref[:, c0:c0 + LANES] = x_ref[c0:c0 + LANES, :].T.astype(BF16)
        acc_ref[...] = jnp.zeros_like(acc_ref)
        stage(h0_ref, None, None, None, True, False, False)

    @pl.when(step == 1)
    def _():
        stage(h1_ref, h0_ref, p1_ref, None, True, True, False)

    mid = (step >= 2) & (step < nblk)

    @pl.when(mid & (step % 2 == 0))
    def _():
        stage(h0_ref, h1_ref, p0_ref, p1_ref, True, True, True)

    @pl.when(mid & (step % 2 == 1))
    def _():
        stage(h1_ref, h0_ref, p1_ref, p0_ref, True, True, True)

    @pl.when(step == nblk)
    def _():
        stage(None, h1_ref, p0_ref, p1_ref, False, True, True)

    @pl.when(step == nblk + 1)
    def _():
        stage(None, None, None, p0_ref, False, False, True)
        for c0 in range(0, tn, LANES):
            rows = slice(c0, c0 + LANES)
            y_ref[rows, :] = _layer_norm(alpha * x_ref[rows, :] + acc_ref[:, rows].T, g_ref[...], b_ref[...])


def _peer_dense(x1, na, bm, e1, e2, u_words, vt_words, ln_g, ln_b, *, tn, ib, alpha):
    t, d = x1.shape
    nk = bm.shape[1]
    nblk = 2 * u_words.shape[0] // (ib * nk)
    assert nblk % 2 == 0 and nblk >= 4
    last = nblk - 1
    weigh_blk = lambda s: jnp.clip(s - 1, 0, last)
    row_code = pl.BlockSpec((PEER_HEADS, ib, tn), lambda i, s: (0, weigh_blk(s), i))
    col_code = pl.BlockSpec((PEER_HEADS, nk, tn), lambda i, s: (0, 0, i))
    return pl.pallas_call(
        functools.partial(_peer_kernel, ib=ib, nblk=nblk, alpha=alpha),
        grid=(t // tn, nblk + 2),
        in_specs=[
            pl.BlockSpec((tn, d), lambda i, s: (i, 0)),
            row_code, col_code, row_code, col_code,
            pl.BlockSpec((ib * nk // 2, d), lambda i, s: (jnp.minimum(s, last), 0)),
            pl.BlockSpec((d // 2, ib * nk), lambda i, s: (0, jnp.clip(s - 2, 0, last))),
            pl.BlockSpec((1, d), lambda i, s: (0, 0)),
            pl.BlockSpec((1, d), lambda i, s: (0, 0)),
        ],
        out_specs=pl.BlockSpec((tn, d), lambda i, s: (i, 0)),
        out_shape=jax.ShapeDtypeStruct((t, d), F32),
        scratch_shapes=[pltpu.VMEM((d, tn), BF16),
                        pltpu.VMEM((ib * nk, tn), F32), pltpu.VMEM((ib * nk, tn), F32),
                        pltpu.VMEM((ib * nk, tn), BF16), pltpu.VMEM((ib * nk, tn), BF16),
                        pltpu.VMEM((d, tn), F32)],
        compiler_params=_cparams(2),
        name="peer_dense",
    )(x1, na, bm, e1, e2, u_words, vt_words, ln_g, ln_b)


def _pick(n, pref):
    return pref if n % pref == 0 else n


def _trunk_layer(x, conv_buf, gla_state, prm, alpha):
    b, l, d = x.shape
    t = b * l
    xf = x.reshape(t, d)
    tm = _pick(t, 256)

    u, qk, v, g, la = _inproj(xf, prm["w_main"], prm["w_f"], prm["gw_pad"], prm["gate_b"],
                              tm=_pick(t, 512))
    conv_dim = u.shape[-1]

    buf_pad = jnp.pad(conv_buf, ((0, 0), (HALO - (CONV_WIDTH - 1), 0), (0, 0)))
    y_conv, conv_new = _conv_mixer(u.reshape(b, l, conv_dim), buf_pad, prm["conv_w"], prm["conv_b"],
                                   prm["conv_ln_g"], prm["conv_ln_b"], tl=_pick(l, 256))

    c = _pick(l, GLA_CHUNK)
    s0_t = jnp.swapaxes(gla_state, -1, -2)
    y_gla, s_t = _gla_mixer(qk.reshape(b, l, -1), v.reshape(b, l, -1), g.reshape(b, l, -1),
                            la.reshape(b, l, -1), s0_t, prm["gla_norm_g"], c=c)
    gla_new = jnp.swapaxes(s_t, -1, -2)

    x1 = _outproj(xf, y_conv.reshape(t, -1), y_gla.reshape(t, -1), prm["w_out_c"], prm["w_out_g"],
                  prm["ln1_g"], prm["ln1_b"], tm=tm, alpha=alpha)

    na, bm, e1, e2 = _route(x1, prm["wq"], prm["keys"], tm=tm)
    y = _peer_dense(x1, na, bm, e1, e2, prm["u_words"], prm["vt_words"], prm["ln2_g"], prm["ln2_b"],
                    tn=_pick(t, 512), ib=8, alpha=alpha)
    return y.reshape(b, l, d), conv_new, gla_new


def _prep_layer(i, w_in, conv_w, conv_b, conv_ln_g, conv_ln_b, gla_gate_w, gla_gate_b, gla_norm_g,
                w_out, ln1_g, ln1_b, peer_w_query, peer_sub_keys, peer_u, peer_v, ln2_g, ln2_b):
    conv_dim = conv_w.shape[-1]
    half = conv_dim // 2
    wi = w_in[i]
    a_w, g_w, rest = wi[:, :conv_dim], wi[:, conv_dim:2 * conv_dim], wi[:, 2 * conv_dim:]
    n_rest = rest.shape[1] - GLA_RANK
    w_main = jnp.concatenate([a_w[:, :half], g_w[:, :half], a_w[:, half:], g_w[:, half:],
                              rest[:, :n_rest]], axis=1).astype(BF16)
    w_f = jnp.pad(rest[:, n_rest:], ((0, 0), (0, LANES - GLA_RANK))).astype(BF16)
    gw_pad = jnp.pad(gla_gate_w[i], ((0, LANES - GLA_RANK), (0, 0)))
    row = lambda a: a[i][None, :]
    keys = peer_sub_keys[i]
    return dict(
        w_main=w_main, w_f=w_f, gw_pad=gw_pad, gate_b=row(gla_gate_b),
        conv_w=conv_w[i], conv_b=row(conv_b), conv_ln_g=row(conv_ln_g), conv_ln_b=row(conv_ln_b),
        gla_norm_g=row(gla_norm_g),
        w_out_c=w_out[i][:conv_dim].astype(BF16), w_out_g=w_out[i][conv_dim:].astype(BF16),
        ln1_g=row(ln1_g), ln1_b=row(ln1_b),
        wq=peer_w_query[i].astype(BF16),
        keys=keys.reshape(keys.shape[0] * 2, keys.shape[2], keys.shape[3]),
        u_words=_pack_table(peer_u[i], transpose=False), vt_words=_pack_table(peer_v[i], transpose=True),
        ln2_g=row(ln2_g), ln2_b=row(ln2_b),
    )


def kernel(x_prompt, x_sample, state_conv, state_gla, w_in, conv_w, conv_b, conv_ln_g, conv_ln_b,
           gla_gate_w, gla_gate_b, gla_norm_g, w_out, ln1_g, ln1_b,
           peer_w_query, peer_sub_keys, peer_u, peer_v, ln2_g, ln2_b):
    depth = w_in.shape[0]
    alpha = (2.0 * depth) ** 0.25
    yp, ys = x_prompt, x_sample
    conv_p, gla_p, conv_s, gla_s = [], [], [], []
    for i in range(depth):
        prm = _prep_layer(i, w_in, conv_w, conv_b, conv_ln_g, conv_ln_b, gla_gate_w, gla_gate_b,
                          gla_norm_g, w_out, ln1_g, ln1_b, peer_w_query, peer_sub_keys,
                          peer_u, peer_v, ln2_g, ln2_b)
        bp = x_prompt.shape[0]
        zero_conv = jnp.zeros((bp,) + state_conv.shape[2:], x_prompt.dtype)
        zero_gla = jnp.zeros((bp,) + state_gla.shape[2:], state_gla.dtype)
        yp, cp, gp = _trunk_layer(yp, zero_conv, zero_gla, prm, alpha)
        ys, cs, gs = _trunk_layer(ys, state_conv[i], state_gla[i], prm, alpha)
        conv_p.append(cp)
        gla_p.append(gp)
        conv_s.append(cs)
        gla_s.append(gs)
    return (yp, ys, jnp.stack(conv_p), jnp.stack(gla_p), jnp.stack(conv_s), jnp.stack(gla_s))
```

```python
import functools

import jax
import jax.numpy as jnp
from jax import lax
from jax.experimental import pallas as pl
from jax.experimental.pallas import tpu as pltpu

F32 = jnp.float32
BF16 = jnp.bfloat16

CONV_WIDTH = 31
GLA_HEADS = 4
GLA_RANK = 16
GLA_TAU = 16.0
PEER_HEADS = 8
PEER_NKEYS = 128
PEER_TOPK = 16
LN_EPS = 1e-5

ROUTE_GROUP = 4
GLA_CHUNK = 64
GLA_SEQS = 4
GLA_SUB = 16
LANES = 128
SUBLANES = 8
HALO = 32
NOT_LISTED = 8192.0
VMEM_LIMIT = 56 * 1024 * 1024


def _cparams(n_axes):
    return pltpu.CompilerParams(dimension_semantics=("arbitrary",) * n_axes,
                                vmem_limit_bytes=VMEM_LIMIT)


def _layer_norm(x, g, b):
    mu = jnp.mean(x, axis=-1, keepdims=True)
    xc = x - mu
    var = jnp.mean(xc * xc, axis=-1, keepdims=True)
    return xc * lax.rsqrt(var + LN_EPS) * g + b


def _sigmoid(x):
    return 1.0 / (1.0 + jnp.exp(-x))


def _log_sigmoid(x):
    return jnp.minimum(x, 0.0) - jnp.log(1.0 + jnp.exp(-jnp.abs(x)))


def _dot_nt(a, b):
    return lax.dot_general(a, b, (((1,), (1,)), ((), ())), preferred_element_type=F32)


def _dot_tn(a, b):
    return lax.dot_general(a, b, (((0,), (0,)), ((), ())), preferred_element_type=F32)


def _pack_kernel(w_ref, o_ref, *, transpose):
    w = w_ref[...]
    o_ref[...] = pltpu.bitcast((w.T if transpose else w).astype(BF16), jnp.uint32)


def _pack_table(w, *, transpose, rows=512):
    r, c = w.shape
    if transpose:
        out_shape, out_block, out_index = (c // 2, r), (c // 2, rows), lambda i: (0, i)
    else:
        out_shape, out_block, out_index = (r // 2, c), (rows // 2, c), lambda i: (i, 0)
    return pl.pallas_call(
        functools.partial(_pack_kernel, transpose=transpose),
        grid=(r // rows,),
        in_specs=[pl.BlockSpec((rows, c), lambda i: (i, 0))],
        out_specs=pl.BlockSpec(out_block, out_index),
        out_shape=jax.ShapeDtypeStruct(out_shape, jnp.uint32),
        compiler_params=_cparams(1),
        name="pack_table",
    )(w)


def _unpack_rows(words_ref, row0, nrows):
    return pltpu.bitcast(words_ref[row0 // 2:(row0 + nrows) // 2, :], BF16)


def _split3(x):
    hi = x.astype(BF16)
    r = x - hi.astype(F32)
    mid = r.astype(BF16)
    lo = (r - mid.astype(F32)).astype(BF16)
    return hi, mid, lo


def _inproj_kernel(x_ref, w_ref, wf_ref, gw_ref, gb_ref,
                   u_ref, qk_ref, v_ref, g_ref, la_ref, xb_ref, *, half):
    xb_ref[...] = x_ref[...].astype(BF16)
    blk = 2 * half
    proj = lambda j: jnp.dot(xb_ref[...], w_ref[:, j * blk:(j + 1) * blk], preferred_element_type=F32)
    for j in range(2):
        z = proj(j)
        u_ref[:, j * half:(j + 1) * half] = z[:, :half] * _sigmoid(z[:, half:])
    qk_ref[...] = proj(2).astype(qk_ref.dtype)
    v_ref[...] = proj(3).astype(v_ref.dtype)
    g_ref[...] = proj(4).astype(g_ref.dtype)
    f = jnp.dot(xb_ref[...], wf_ref[...], preferred_element_type=F32)
    pre = jnp.dot(f, gw_ref[...], preferred_element_type=F32,
                  precision=lax.Precision.HIGHEST) + gb_ref[...]
    la_ref[...] = _log_sigmoid(pre) * (1.0 / GLA_TAU)


def _inproj(x, w_main, w_f, gw_pad, gb, *, tm):
    t, d = x.shape
    conv_dim = 1024
    qk2 = 1024
    v_dim = 1024
    assert w_main.shape[1] == 2 * conv_dim + qk2 + 2 * v_dim
    resident = dict(pipeline_mode=pl.Buffered(1))
    out_shapes = (
        jax.ShapeDtypeStruct((t, conv_dim), F32),
        jax.ShapeDtypeStruct((t, qk2), BF16),
        jax.ShapeDtypeStruct((t, v_dim), BF16),
        jax.ShapeDtypeStruct((t, v_dim), BF16),
        jax.ShapeDtypeStruct((t, qk2 // 2), F32),
    )
    row = lambda i: (i, 0)
    fixed = lambda i: (0, 0)
    return pl.pallas_call(
        functools.partial(_inproj_kernel, half=conv_dim // 2),
        grid=(t // tm,),
        in_specs=[
            pl.BlockSpec((tm, d), row),
            pl.BlockSpec(w_main.shape, fixed, **resident),
            pl.BlockSpec((d, LANES), fixed, **resident),
            pl.BlockSpec((LANES, qk2 // 2), fixed, **resident),
            pl.BlockSpec((1, qk2 // 2), fixed, **resident),
        ],
        out_specs=[
            pl.BlockSpec((tm, conv_dim), row),
            pl.BlockSpec((tm, qk2), row),
            pl.BlockSpec((tm, v_dim), row),
            pl.BlockSpec((tm, v_dim), row),
            pl.BlockSpec((tm, qk2 // 2), row),
        ],
        out_shape=out_shapes,
        scratch_shapes=[pltpu.VMEM((tm, d), BF16)],
        compiler_params=_cparams(1),
        name="inproj",
    )(x, w_main, w_f, gw_pad, gb)


def _conv_kernel(buf_ref, u_ref, w_ref, cb_ref, lg_ref, lb_ref, y_ref, st_ref, ext_ref, sh_ref, wb_ref,
                 *, tl, rs):
    l = pl.program_id(1)

    @pl.when(l == 0)
    def _():
        ext_ref[0:HALO, :] = buf_ref[0]

    @pl.when(l > 0)
    def _():
        ext_ref[0:HALO, :] = ext_ref[tl:tl + HALO, :]

    ext_ref[HALO:HALO + tl, :] = u_ref[0]
    first = HALO - (CONV_WIDTH - 1)
    span = sh_ref.shape[1]
    for r in range(1, SUBLANES):
        sh_ref[r - 1] = ext_ref[r:r + span, :]

    def window(row, n):
        shift = row % SUBLANES
        base = row - shift
        return ext_ref[base:base + n, :] if shift == 0 else sh_ref[shift - 1, base:base + n, :]

    @pl.when((pl.program_id(0) == 0) & (l == 0))
    def _():
        for j in range(CONV_WIDTH):
            wb_ref[j] = jnp.broadcast_to(w_ref[j:j + 1, :], wb_ref.shape[1:])

    for r0 in range(0, tl, rs):
        acc = window(first + r0, rs) * wb_ref[0]
        for j in range(1, CONV_WIDTH):
            acc = acc + window(first + r0 + j, rs) * wb_ref[j]
        y = _layer_norm(acc + cb_ref[...], lg_ref[...], lb_ref[...])
        y_ref[0, r0:r0 + rs, :] = (y * _sigmoid(y)).astype(y_ref.dtype)
    st_ref[0] = ext_ref[tl + first:tl + HALO, :]


def _conv_mixer(u, buf_pad, conv_w, conv_b, ln_g, ln_b, *, tl):
    b, l, c = u.shape
    rs = min(16, tl)
    fixed = lambda i, j: (0, 0)
    return pl.pallas_call(
        functools.partial(_conv_kernel, tl=tl, rs=rs),
        grid=(b, l // tl),
        in_specs=[
            pl.BlockSpec((1, HALO, c), lambda i, j: (i, 0, 0)),
            pl.BlockSpec((1, tl, c), lambda i, j: (i, j, 0)),
            pl.BlockSpec((CONV_WIDTH, c), fixed),
            pl.BlockSpec((1, c), fixed),
            pl.BlockSpec((1, c), fixed),
            pl.BlockSpec((1, c), fixed),
        ],
        out_specs=[
            pl.BlockSpec((1, tl, c), lambda i, j: (i, j, 0)),
            pl.BlockSpec((1, CONV_WIDTH - 1, c), lambda i, j: (i, 0, 0)),
        ],
        out_shape=(jax.ShapeDtypeStruct((b, l, c), BF16),
                   jax.ShapeDtypeStruct((b, CONV_WIDTH - 1, c), F32)),
        scratch_shapes=[pltpu.VMEM((HALO + tl, c), F32),
                        pltpu.VMEM((SUBLANES - 1, tl + HALO - SUBLANES, c), F32),
                        pltpu.VMEM((CONV_WIDTH, rs, c), F32)],
        compiler_params=_cparams(2),
        name="conv_mixer",
    )(buf_pad, u, conv_w, conv_b, ln_g, ln_b)


def _gla_kernel(qk_ref, v_ref, g_ref, la_ref, s0_ref, ng_ref, y_ref, sout_ref, st_ref, *, c, dk, dv):
    l = pl.program_id(1)

    @pl.when(l == 0)
    def _():
        st_ref[...] = s0_ref[...]

    for seq in range(qk_ref.shape[0]):
        _gla_chunk(seq, qk_ref, v_ref, g_ref, la_ref, ng_ref, y_ref, st_ref, c=c, dk=dk, dv=dv)
    sout_ref[...] = st_ref[...]


def _gla_chunk(seq, qk_ref, v_ref, g_ref, la_ref, ng_ref, y_ref, st_ref, *, c, dk, dv):
    hk = GLA_HEADS * dk
    la = la_ref[seq]
    rows = lax.broadcasted_iota(jnp.int32, (c, c), 0)
    cols = lax.broadcasted_iota(jnp.int32, (c, c), 1)
    causal = rows >= cols
    tril = jnp.where(causal, 1.0, 0.0).astype(BF16)
    hi, mid, lo = _split3(la)
    bcum = (jnp.dot(tril, hi, preferred_element_type=F32)
            + jnp.dot(tril, mid, preferred_element_type=F32)
            + jnp.dot(tril, lo, preferred_element_type=F32))
    b_last = bcum[c - 1:c, :]
    q = qk_ref[seq, :, :hk].astype(F32) * (dk ** -0.5)
    k = qk_ref[seq, :, hk:].astype(F32)
    q_in = (q * jnp.exp(bcum)).astype(BF16)
    k_out = (k * jnp.exp(b_last - bcum)).astype(BF16)
    row_id = lax.broadcasted_iota(jnp.int32, (c, dk), 0)

    for h in range(GLA_HEADS):
        ks = slice(h * dk, (h + 1) * dk)
        vs = slice(h * dv, (h + 1) * dv)
        bh = bcum[:, ks]
        qh = q[:, ks]
        kh = k[:, ks]
        vh = v_ref[seq, :, vs].astype(BF16)
        st = st_ref[seq, h]
        o = _dot_nt(q_in[:, ks], st.astype(BF16))
        blocks = []
        for i in range(c // GLA_SUB):
            r0 = i * GLA_SUB
            base = bh[r0 - 1:r0, :] if i > 0 else jnp.zeros((1, dk), F32)
            q_t = qh[r0:r0 + GLA_SUB, :] * jnp.exp(bh[r0:r0 + GLA_SUB, :] - base)
            k_t = kh * jnp.exp(jnp.where(row_id < r0 + GLA_SUB, base - bh, -1e30))
            blocks.append(_dot_nt(q_t.astype(BF16), k_t.astype(BF16)))
        scores = jnp.where(causal, jnp.concatenate(blocks, axis=0), 0.0)
        o = o + jnp.dot(scores.astype(BF16), vh, preferred_element_type=F32)
        st_ref[seq, h] = st * jnp.exp(b_last[:, ks]) + _dot_tn(vh, k_out[:, ks])
        o = o * lax.rsqrt(jnp.mean(o * o, axis=-1, keepdims=True) + LN_EPS)
        gh = g_ref[seq, :, vs].astype(F32)
        y_ref[seq, :, vs] = (o * ng_ref[:, vs] * (gh * _sigmoid(gh))).astype(y_ref.dtype)


def _gla_mixer(qk, v, g, la, s0_t, norm_g, *, c):
    b, l, hk2 = qk.shape
    nb = GLA_SEQS if b % GLA_SEQS == 0 else 1
    dk = hk2 // (2 * GLA_HEADS)
    dv = v.shape[-1] // GLA_HEADS
    tile = lambda i, j: (i, j, 0)
    state = pl.BlockSpec((nb, GLA_HEADS, dv, dk), lambda i, j: (i, 0, 0, 0))
    return pl.pallas_call(
        functools.partial(_gla_kernel, c=c, dk=dk, dv=dv),
        grid=(b // nb, l // c),
        in_specs=[
            pl.BlockSpec((nb, c, hk2), tile),
            pl.BlockSpec((nb, c, GLA_HEADS * dv), tile),
            pl.BlockSpec((nb, c, GLA_HEADS * dv), tile),
            pl.BlockSpec((nb, c, hk2 // 2), tile),
            state,
            pl.BlockSpec((1, GLA_HEADS * dv), lambda i, j: (0, 0)),
        ],
        out_specs=[
            pl.BlockSpec((nb, c, GLA_HEADS * dv), tile),
            state,
        ],
        out_shape=(jax.ShapeDtypeStruct((b, l, GLA_HEADS * dv), BF16),
                   jax.ShapeDtypeStruct((b, GLA_HEADS, dv, dk), F32)),
        scratch_shapes=[pltpu.VMEM((nb, GLA_HEADS, dv, dk), F32)],
        compiler_params=_cparams(2),
        name="gla_mixer",
    )(qk, v, g, la, s0_t, norm_g)


def _outproj_kernel(x_ref, yc_ref, yg_ref, wc_ref, wg_ref, g_ref, b_ref, o_ref, *, alpha):
    mix = (jnp.dot(yc_ref[...].astype(BF16), wc_ref[...], preferred_element_type=F32)
           + jnp.dot(yg_ref[...].astype(BF16), wg_ref[...], preferred_element_type=F32))
    o_ref[...] = _layer_norm(alpha * x_ref[...] + mix, g_ref[...], b_ref[...])


def _outproj(x, y_conv, y_gla, w_c, w_g, ln_g, ln_b, *, tm, alpha):
    t, d = x.shape
    row = lambda i: (i, 0)
    fixed = lambda i: (0, 0)
    return pl.pallas_call(
        functools.partial(_outproj_kernel, alpha=alpha),
        grid=(t // tm,),
        in_specs=[
            pl.BlockSpec((tm, d), row),
            pl.BlockSpec((tm, y_conv.shape[1]), row),
            pl.BlockSpec((tm, y_gla.shape[1]), row),
            pl.BlockSpec(w_c.shape, fixed),
            pl.BlockSpec(w_g.shape, fixed),
            pl.BlockSpec((1, d), fixed),
            pl.BlockSpec((1, d), fixed),
        ],
        out_specs=pl.BlockSpec((tm, d), row),
        out_shape=jax.ShapeDtypeStruct((t, d), F32),
        compiler_params=_cparams(1),
        name="outproj_ln",
    )(x, y_conv, y_gla, w_c, w_g, ln_g, ln_b)


def _extract_topk(s):
    n = s.shape[0]
    rows = lax.broadcasted_iota(jnp.int32, s.shape, 0).astype(F32)
    rank = jnp.full(s.shape, NOT_LISTED, F32)
    vals = []
    for r in range(PEER_TOPK):
        m = jnp.max(s, axis=0, keepdims=True)
        first = jnp.min(jnp.where(s == m, rows, float(n)), axis=0, keepdims=True)
        hit = rows == first
        rank = jnp.where(hit, float(r), rank)
        s = jnp.where(hit, -jnp.inf, s)
        vals.append(m)
    return vals, rank


def _sort_pairs(lo, hi):
    def merge(lo, hi, r):
        step = r * 2
        if step < hi - lo:
            yield from merge(lo, hi, step)
            yield from merge(lo + r, hi, step)
            yield from [(i, i + r) for i in range(lo + r, hi - r, step)]
        else:
            yield (lo, lo + r)

    if hi - lo >= 1:
        mid = lo + (hi - lo) // 2
        yield from _sort_pairs(lo, mid)
        yield from _sort_pairs(mid + 1, hi)
        yield from merge(lo, hi, 1)


def _topk_values(s):
    k = PEER_TOPK
    v = [s[r:r + SUBLANES, :] for r in range(0, s.shape[0], SUBLANES)]
    assert len(v) >= k and len(v) & (len(v) - 1) == 0
    for i, j in _sort_pairs(0, len(v) - 1):
        v[i], v[j] = jnp.maximum(v[i], v[j]), jnp.minimum(v[i], v[j])
    vals = []
    taken = jnp.zeros(v[0].shape, F32)
    for t in range(k):
        m = jnp.max(v[0], axis=0, keepdims=True)
        hit = v[0] == m
        taken = taken + jnp.where(hit, 1.0, 0.0)
        for r in range(k - 1 - t):
            v[r] = jnp.where(hit, v[r + 1], v[r])
        vals.append(m)
    equal = jnp.sum(taken, axis=0, keepdims=True) != float(k)
    for t in range(k - 1):
        equal = equal | (vals[t] == vals[t + 1])
    at_least = jnp.sum(jnp.where(s >= vals[k - 1], 1.0, 0.0), axis=0, keepdims=True)
    return vals, equal | (at_least != float(k))


def _dup16(x):
    bits = lax.bitcast_convert_type(x.astype(BF16).astype(F32), jnp.uint32)
    return lax.bitcast_convert_type(bits | (bits >> 16), jnp.int32)


def _route_head(s1, s2, exact):
    k = PEER_TOPK
    assert k == 16
    if exact:
        vals1, rank1 = _extract_topk(s1)
        vals2, rank2 = _extract_topk(s2)
        unusable = None
    else:
        vals1, equal1 = _topk_values(s1)
        vals2, equal2 = _topk_values(s2)
        unusable = equal1 | equal2
    v2 = jnp.concatenate(vals2, axis=0)
    v1_tail = jnp.concatenate(vals1[8:], axis=0)
    ev2 = jnp.exp(v2 - vals2[0])
    ev1 = [jnp.exp(vals1[a] - vals1[0]) for a in range(8)]
    ev1_tail = jnp.exp(v1_tail - vals1[0])
    cand = jnp.concatenate([vals1[0] + v2] + [vals1[a] + v2[0:8] for a in range(1, 8)]
                           + [v1_tail + vals2[0]], axis=0)
    gate = jnp.concatenate([ev1[0] * ev2] + [ev1[a] * ev2[0:8] for a in range(1, 8)]
                           + [ev1_tail * ev2[0:1]], axis=0)
    r = lax.broadcasted_iota(jnp.int32, cand.shape, 0)
    a_mid = 1 + ((r - 16) >> 3)
    b_mid = (r - 16) & 7
    head_rows = r < 16
    tail_rows = r >= 72
    flat = jnp.where(head_rows, r, jnp.where(tail_rows, (r - 64) * k, a_mid * k + b_mid)).astype(F32)
    valid = head_rows | tail_rows | ((a_mid + 1) * (b_mid + 1) <= k)
    cand = jnp.where(valid, cand, -jnp.inf)
    sel = jnp.zeros(cand.shape, F32)
    for _ in range(k):
        m = jnp.max(cand, axis=0, keepdims=True)
        if exact:
            first = jnp.min(jnp.where(cand == m, flat, float(k * k)), axis=0, keepdims=True)
            hit = flat == first
        else:
            hit = cand == m
        sel = jnp.where(hit, 1.0, sel)
        cand = jnp.where(hit, -jnp.inf, cand)
    z = jnp.sum(sel * gate, axis=0, keepdims=True)
    heights = [jnp.sum(sel[0:16], axis=0, keepdims=True)]
    heights += [jnp.sum(sel[8 + 8 * a:16 + 8 * a], axis=0, keepdims=True) for a in range(1, 8)]
    heights += [sel[64 + a:65 + a] for a in range(8, 16)]
    na = jnp.full(s1.shape, NOT_LISTED, F32)
    bm = jnp.full(s2.shape, -2.0 * NOT_LISTED, F32)
    for a in range(k):
        na = jnp.where((rank1 == float(a)) if exact else (s1 == vals1[a]), 1.0 - heights[a], na)
        bm = jnp.where((rank2 == float(a)) if exact else (s2 == vals2[a]), -float(a), bm)
    e1 = jnp.exp(s1 - vals1[0]) / z
    e2 = jnp.exp(s2 - vals2[0])
    if not exact:
        unusable = unusable | (jnp.sum(sel, axis=0, keepdims=True) != float(k))
    return na, bm, e1, e2, unusable


def _route_kernel(x_ref, wq_ref, keys_ref, na_ref, bm_ref, e1_ref, e2_ref, s_ref):
    xq = jnp.dot(x_ref[...].astype(BF16), wq_ref[...], preferred_element_type=F32)
    dsub = keys_ref.shape[-1]
    for sub in range(2 * PEER_HEADS):
        qh, qm, ql = _split3(xq[:, sub * dsub:(sub + 1) * dsub])
        kh, km, kl = _split3(keys_ref[sub])
        s_ref[sub] = (_dot_nt(kh, qh) + (_dot_nt(kh, qm) + _dot_nt(km, qh))
                      + (_dot_nt(kh, ql) + _dot_nt(km, qm) + _dot_nt(kl, qh)))

    def emit(h, exact):
        na, bm, e1, e2, unusable = _route_head(s_ref[2 * h], s_ref[2 * h + 1], exact)
        na_ref[h] = _dup16(na)
        bm_ref[h] = bm.astype(BF16)
        e1_ref[h] = _dup16(e1)
        e2_ref[h] = e2.astype(BF16)
        return unusable

    for h0 in range(0, PEER_HEADS, ROUTE_GROUP):
        group = range(h0, h0 + ROUTE_GROUP)
        unusable = functools.reduce(jnp.logical_or, [emit(h, False) for h in group])

        @pl.when(jnp.sum(jnp.where(unusable, 1.0, 0.0)) > 0.0)
        def _():
            for h in group:
                emit(h, True)


def _route(x1, wq, keys, *, tm):
    t, d = x1.shape
    nk = keys.shape[1]
    ospec = pl.BlockSpec((PEER_HEADS, nk, tm), lambda i: (0, 0, i))
    words = jax.ShapeDtypeStruct((PEER_HEADS, nk, t), jnp.int32)
    halfs = jax.ShapeDtypeStruct((PEER_HEADS, nk, t), BF16)
    return pl.pallas_call(
        _route_kernel,
        grid=(t // tm,),
        in_specs=[
            pl.BlockSpec((tm, d), lambda i: (i, 0)),
            pl.BlockSpec(wq.shape, lambda i: (0, 0)),
            pl.BlockSpec(keys.shape, lambda i: (0, 0, 0)),
        ],
        out_specs=[ospec] * 4,
        out_shape=(words, halfs, words, halfs),
        scratch_shapes=[pltpu.VMEM((2 * PEER_HEADS, nk, tm), F32)],
        compiler_params=_cparams(1),
        name="peer_route",
    )(x1, wq, keys)


def _gelu(x):
    return 0.5 * x * (1.0 + lax.erf(x * 0.7071067811865476))


def _peer_kernel(x_ref, na_ref, bm_ref, e1_ref, e2_ref, u_ref, vt_ref, g_ref, b_ref, y_ref,
                 xt_ref, h0_ref, h1_ref, p0_ref, p1_ref, acc_ref, *, ib, nblk, alpha):
    step = pl.program_id(1)
    nk = bm_ref.shape[1]
    tn = xt_ref.shape[1]

    def stage(h_new, h_old, p_new, p_old, project, weigh, combine):
        parts = 4
        group = 16
        hm = ib * nk // parts
        dm = acc_ref.shape[0] // parts
        for q in range(parts):
            if project:
                h_new[q * hm:(q + 1) * hm, :] = jnp.dot(_unpack_rows(u_ref, q * hm, hm), xt_ref[...],
                                                        preferred_element_type=F32)
            if weigh:
                for ii in range(q * ib // parts, (q + 1) * ib // parts):
                    row = lambda ref, h: pltpu.bitcast(
                        jnp.broadcast_to(ref[h, ii:ii + 1, :], (group // 2, tn)), BF16)
                    na_b = [row(na_ref, h) for h in range(PEER_HEADS)]
                    e1_b = [row(e1_ref, h) for h in range(PEER_HEADS)]
                    for r0 in range(0, nk, group):
                        cols = slice(r0, r0 + group)
                        w = jnp.zeros((group, tn), BF16)
                        for h in range(PEER_HEADS):
                            w = w + jnp.where(bm_ref[h, cols, :] >= na_b[h], e2_ref[h, cols, :] * e1_b[h],
                                              jnp.zeros((group, tn), BF16))
                        rows = slice(ii * nk + r0, ii * nk + r0 + group)
                        p_new[rows, :] = _gelu(h_old[rows, :]).astype(BF16) * w
            if combine:
                acc_ref[q * dm:(q + 1) * dm, :] += jnp.dot(_unpack_rows(vt_ref, q * dm, dm), p_old[...],
                                                           preferred_element_type=F32)

    @pl.when(step == 0)
    def _():
        for c0 in range(0, tn, LANES):
            xt_ref[:, c0:c0 + LANES] = x_ref[c0:c0 + LANES, :].T.astype(BF16)
        acc_ref[...] = jnp.zeros_like(acc_ref)
        stage(h0_ref, None, None, None, True, False, False)

    @pl.when(step == 1)
    def _():
        stage(h1_ref, h0_ref, p1_ref, None, True, True, False)

    mid = (step >= 2) & (step < nblk)

    @pl.when(mid & (step % 2 == 0))
    def _():
        stage(h0_ref, h1_ref, p0_ref, p1_ref, True, True, True)

    @pl.when(mid & (step % 2 == 1))
    def _():
        stage(h1_ref, h0_ref, p1_ref, p0_ref, True, True, True)

    @pl.when(step == nblk)
    def _():
        stage(None, h1_ref, p0_ref, p1_ref, False, True, True)

    @pl.when(step == nblk + 1)
    def _():
        stage(None, None, None, p0_ref, False, False, True)
        for c0 in range(0, tn, LANES):
            rows = slice(c0, c0 + LANES)
            y_ref[rows, :] = _layer_norm(alpha * x_ref[rows, :] + acc_ref[:, rows].T, g_ref[...], b_ref[...])


def _peer_dense(x1, na, bm, e1, e2, u_words, vt_words, ln_g, ln_b, *, tn, ib, alpha):
    t, d = x1.shape
    nk = bm.shape[1]
    nblk = 2 * u_words.shape[0] // (ib * nk)
    assert nblk % 2 == 0 and nblk >= 4
    last = nblk - 1
    weigh_blk = lambda s: jnp.clip(s - 1, 0, last)
    row_code = pl.BlockSpec((PEER_HEADS, ib, tn), lambda i, s: (0, weigh_blk(s), i))
    col_code = pl.BlockSpec((PEER_HEADS, nk, tn), lambda i, s: (0, 0, i))
    return pl.pallas_call(
        functools.partial(_peer_kernel, ib=ib, nblk=nblk, alpha=alpha),
        grid=(t // tn, nblk + 2),
        in_specs=[
            pl.BlockSpec((tn, d), lambda i, s: (i, 0)),
            row_code, col_code, row_code, col_code,
            pl.BlockSpec((ib * nk // 2, d), lambda i, s: (jnp.minimum(s, last), 0)),
            pl.BlockSpec((d // 2, ib * nk), lambda i, s: (0, jnp.clip(s - 2, 0, last))),
            pl.BlockSpec((1, d), lambda i, s: (0, 0)),
            pl.BlockSpec((1, d), lambda i, s: (0, 0)),
        ],
        out_specs=pl.BlockSpec((tn, d), lambda i, s: (i, 0)),
        out_shape=jax.ShapeDtypeStruct((t, d), F32),
        scratch_shapes=[pltpu.VMEM((d, tn), BF16),
                        pltpu.VMEM((ib * nk, tn), F32), pltpu.VMEM((ib * nk, tn), F32),
                        pltpu.VMEM((ib * nk, tn), BF16), pltpu.VMEM((ib * nk, tn), BF16),
                        pltpu.VMEM((d, tn), F32)],
        compiler_params=_cparams(2),
        name="peer_dense",
    )(x1, na, bm, e1, e2, u_words, vt_words, ln_g, ln_b)


def _pick(n, pref):
    return pref if n % pref == 0 else n


def _trunk_layer(x, conv_buf, gla_state, prm, alpha):
    b, l, d = x.shape
    t = b * l
    xf = x.reshape(t, d)
    tm = _pick(t, 256)

    u, qk, v, g, la = _inproj(xf, prm["w_main"], prm["w_f"], prm["gw_pad"], prm["gate_b"],
                              tm=_pick(t, 512))
    conv_dim = u.shape[-1]

    buf_pad = jnp.pad(conv_buf, ((0, 0), (HALO - (CONV_WIDTH - 1), 0), (0, 0)))
    y_conv, conv_new = _conv_mixer(u.reshape(b, l, conv_dim), buf_pad, prm["conv_w"], prm["conv_b"],
                                   prm["conv_ln_g"], prm["conv_ln_b"], tl=_pick(l, 256))

    c = _pick(l, GLA_CHUNK)
    s0_t = jnp.swapaxes(gla_state, -1, -2)
    y_gla, s_t = _gla_mixer(qk.reshape(b, l, -1), v.reshape(b, l, -1), g.reshape(b, l, -1),
                            la.reshape(b, l, -1), s0_t, prm["gla_norm_g"], c=c)
    gla_new = jnp.swapaxes(s_t, -1, -2)

    x1 = _outproj(xf, y_conv.reshape(t, -1), y_gla.reshape(t, -1), prm["w_out_c"], prm["w_out_g"],
                  prm["ln1_g"], prm["ln1_b"], tm=tm, alpha=alpha)

    na, bm, e1, e2 = _route(x1, prm["wq"], prm["keys"], tm=tm)
    y = _peer_dense(x1, na, bm, e1, e2, prm["u_words"], prm["vt_words"], prm["ln2_g"], prm["ln2_b"],
                    tn=_pick(t, 512), ib=8, alpha=alpha)
    return y.reshape(b, l, d), conv_new, gla_new


def _prep_layer(i, w_in, conv_w, conv_b, conv_ln_g, conv_ln_b, gla_gate_w, gla_gate_b, gla_norm_g,
                w_out, ln1_g, ln1_b, peer_w_query, peer_sub_keys, peer_u, peer_v, ln2_g, ln2_b):
    conv_dim = conv_w.shape[-1]
    half = conv_dim // 2
    wi = w_in[i]
    a_w, g_w, rest = wi[:, :conv_dim], wi[:, conv_dim:2 * conv_dim], wi[:, 2 * conv_dim:]
    n_rest = rest.shape[1] - GLA_RANK
    w_main = jnp.concatenate([a_w[:, :half], g_w[:, :half], a_w[:, half:], g_w[:, half:],
                              rest[:, :n_rest]], axis=1).astype(BF16)
    w_f = jnp.pad(rest[:, n_rest:], ((0, 0), (0, LANES - GLA_RANK))).astype(BF16)
    gw_pad = jnp.pad(gla_gate_w[i], ((0, LANES - GLA_RANK), (0, 0)))
    row = lambda a: a[i][None, :]
    keys = peer_sub_keys[i]
    return dict(
        w_main=w_main, w_f=w_f, gw_pad=gw_pad, gate_b=row(gla_gate_b),
        conv_w=conv_w[i], conv_b=row(conv_b), conv_ln_g=row(conv_ln_g), conv_ln_b=row(conv_ln_b),
        gla_norm_g=row(gla_norm_g),
        w_out_c=w_out[i][:conv_dim].astype(BF16), w_out_g=w_out[i][conv_dim:].astype(BF16),
        ln1_g=row(ln1_g), ln1_b=row(ln1_b),
        wq=peer_w_query[i].astype(BF16),
        keys=keys.reshape(keys.shape[0] * 2, keys.shape[2], keys.shape[3]),
        u_words=_pack_table(peer_u[i], transpose=False), vt_words=_pack_table(peer_v[i], transpose=True),
        ln2_g=row(ln2_g), ln2_b=row(ln2_b),
    )


def kernel(x_prompt, x_sample, state_conv, state_gla, w_in, conv_w, conv_b, conv_ln_g, conv_ln_b,
           gla_gate_w, gla_gate_b, gla_norm_g, w_out, ln1_g, ln1_b,
           peer_w_query, peer_sub_keys, peer_u, peer_v, ln2_g, ln2_b):
    depth = w_in.shape[0]
    alpha = (2.0 * depth) ** 0.25
    yp, ys = x_prompt, x_sample
    conv_p, gla_p, conv_s, gla_s = [], [], [], []
    for i in range(depth):
        prm = _prep_layer(i, w_in, conv_w, conv_b, conv_ln_g, conv_ln_b, gla_gate_w, gla_gate_b,
                          gla_norm_g, w_out, ln1_g, ln1_b, peer_w_query, peer_sub_keys,
                          peer_u, peer_v, ln2_g, ln2_b)
        bp = x_prompt.shape[0]
        zero_conv = jnp.zeros((bp,) + state_conv.shape[2:], x_prompt.dtype)
        zero_gla = jnp.zeros((bp,) + state_gla.shape[2:], state_gla.dtype)
        yp, cp, gp = _trunk_layer(yp, zero_conv, zero_gla, prm, alpha)
        ys, cs, gs = _trunk_layer(ys, state_conv[i], state_gla[i], prm, alpha)
        conv_p.append(cp)
        gla_p.append(gp)
        conv_s.append(cs)
        gla_s.append(gs)
    return (yp, ys, jnp.stack(conv_p), jnp.stack(gla_p), jnp.stack(conv_s), jnp.stack(gla_s))
```
